```python
import jax, jax.numpy as jnp
from jax import lax
import numpy as np

D_MODEL = 1024
BATCH = 8
SEQ = 4096
DEPTH = 2

CHUNK = 128
D_SGU = D_MODEL
SGU_GROUPS = 8
SGU_GC = D_SGU // SGU_GROUPS
D_POOL = D_MODEL
POOL_WINDOWS = (2, 4, 8, 16)
POOL_GROUPS = len(POOL_WINDOWS)
POOL_GC = D_POOL // POOL_GROUPS
D_CONV = D_MODEL
CONV_WIDTH = 31
N_BRANCH = 3
D_FF = -(-8 * D_MODEL // (3 * 256)) * 256
EPS = 1e-6

OFF_U = 0
OFF_V = OFF_U + D_SGU
OFF_POOL = OFF_V + D_SGU
OFF_GLU = OFF_POOL + D_POOL
OFF_GATE = OFF_GLU + 2 * D_CONV
D_IN = OFF_GATE + N_BRANCH * D_MODEL

kernel_name = "hybrid_sgu_pool_conv_adaln_block"


def rmsnorm(x, g):
    xf = x.astype(jnp.float32)
    y = xf * lax.rsqrt(jnp.mean(xf * xf, axis=-1, keepdims=True) + EPS)
    return (y * g.astype(jnp.float32)).astype(x.dtype)


def layernorm(x, g, b):
    xf = x.astype(jnp.float32)
    mu = jnp.mean(xf, axis=-1, keepdims=True)
    var = jnp.mean(jnp.square(xf - mu), axis=-1, keepdims=True)
    y = (xf - mu) * lax.rsqrt(var + EPS) * g.astype(jnp.float32) + b.astype(jnp.float32)
    return y.astype(x.dtype)


def spatial_gating(u, v, ln_g, ln_b, w_s, b_s):
    B, S, _ = v.shape
    v = layernorm(v, ln_g, ln_b)
    vc = v.reshape(B, S // CHUNK, CHUNK, SGU_GROUPS, SGU_GC)
    mask = jnp.tril(jnp.ones((CHUNK, CHUNK), dtype=w_s.dtype))
    w = w_s * mask
    mixed = jnp.einsum('gts,bnsgc->bntgc', w, vc) + jnp.transpose(b_s)[None, None, :, :, None]
    return u * mixed.reshape(B, S, D_SGU)


def multiscale_pool(p, w_pool, pool_scale):
    B, S, _ = p.shape
    pg = p.reshape(B, S, POOL_GROUPS, POOL_GC).astype(jnp.float32)
    cs = jnp.cumsum(pg, axis=1)
    t = jnp.arange(1, S + 1, dtype=jnp.float32)
    outs = []
    for i, win in enumerate(POOL_WINDOWS):
        c_i = cs[:, :, i]
        lag = jnp.pad(c_i, ((0, 0), (win, 0), (0, 0)))[:, :S]
        cnt = jnp.minimum(t, float(win))[None, :, None]
        outs.append((c_i - lag) / cnt - pg[:, :, i])
    pooled = jnp.stack(outs, axis=2).astype(p.dtype)
    y = jnp.einsum('bsgc,gcd->bsgd', pooled, w_pool).reshape(B, S, D_POOL)
    return y * pool_scale


def conformer_conv(a, a_gate, conv_w, conv_b, ln_g, ln_b):
    z = a * jax.nn.sigmoid(a_gate)
    z = lax.conv_general_dilated(
        z, conv_w[:, None, :], window_strides=(1,), padding=[(CONV_WIDTH - 1, 0)],
        dimension_numbers=('NWC', 'WIO', 'NWC'), feature_group_count=D_CONV) + conv_b
    z = layernorm(z, ln_g, ln_b)
    return jax.nn.silu(z)


def setup_inputs(seed: int = 0) -> dict:
    key = jax.random.key(seed)
    ks = jax.random.split(key, 32)
    L, D = DEPTH, D_MODEL

    def nrm(k, shape, scale):
        return jax.random.normal(k, shape, dtype=jnp.float32) * scale

    return {
        "x": nrm(ks[0], (BATCH, SEQ, D), 1.0),
        "c": nrm(ks[1], (BATCH, D), 1.0),
        "w_ada": nrm(ks[2], (L, D, 6 * D), 0.5 * D ** -0.5),
        "b_ada": nrm(ks[3], (L, 6 * D), 0.01),
        "g_mix": 1.0 + nrm(ks[4], (L, D), 0.1),
        "w_in": nrm(ks[5], (L, D, D_IN), D ** -0.5),
        "b_in": nrm(ks[6], (L, D_IN), 0.01),
        "sgu_ln_g": 1.0 + nrm(ks[7], (L, D_SGU), 0.1),
        "sgu_ln_b": nrm(ks[8], (L, D_SGU), 0.01),
        "sgu_w_s": nrm(ks[9], (L, SGU_GROUPS, CHUNK, CHUNK), 0.5 * CHUNK ** -0.5),
        "sgu_b_s": 1.0 + nrm(ks[10], (L, SGU_GROUPS, CHUNK), 0.1),
        "w_pa": nrm(ks[11], (L, D_SGU, D), D_SGU ** -0.5),
        "pool_w": nrm(ks[12], (L, POOL_GROUPS, POOL_GC, POOL_GC), POOL_GC ** -0.5),
        "pool_scale": 1.0 + nrm(ks[13], (L, D_POOL), 0.1),
        "w_pb": nrm(ks[14], (L, D_POOL, D), D_POOL ** -0.5),
        "conv_w": nrm(ks[15], (L, CONV_WIDTH, D_CONV), CONV_WIDTH ** -0.5),
        "conv_b": nrm(ks[16], (L, D_CONV), 0.01),
        "conv_ln_g": 1.0 + nrm(ks[17], (L, D_CONV), 0.1),
        "conv_ln_b": nrm(ks[18], (L, D_CONV), 0.01),
        "w_pc": nrm(ks[19], (L, D_CONV, D), D_CONV ** -0.5),
        "w_out": nrm(ks[20], (L, D, D), D ** -0.5),
        "g_ffn": 1.0 + nrm(ks[21], (L, D), 0.1),
        "w_ffn_in": nrm(ks[22], (L, D, 2 * D_FF), D ** -0.5),
        "w_ffn_out": nrm(ks[23], (L, D_FF, D), D_FF ** -0.5),
        "g_final": 1.0 + nrm(ks[24], (D,), 0.1),
    }


def reference(x, c, w_ada, b_ada, g_mix, w_in, b_in, sgu_ln_g, sgu_ln_b, sgu_w_s, sgu_b_s, w_pa,
              pool_w, pool_scale, w_pb, conv_w, conv_b, conv_ln_g, conv_ln_b, w_pc, w_out,
              g_ffn, w_ffn_in, w_ffn_out, g_final):
    B, S, D = x.shape
    c_act = jax.nn.silu(c)
    for l in range(DEPTH):
        ada = (c_act @ w_ada[l] + b_ada[l])[:, None, :]
        sh_m, sc_m, gt_m, sh_f, sc_f, gt_f = jnp.split(ada, 6, axis=-1)

        h = rmsnorm(x, g_mix[l]) * (1.0 + sc_m) + sh_m
        z = h @ w_in[l] + b_in[l]
        u = jax.nn.gelu(z[..., OFF_U:OFF_V], approximate=False)
        v = jax.nn.gelu(z[..., OFF_V:OFF_POOL], approximate=False)
        p = z[..., OFF_POOL:OFF_GLU]
        a = z[..., OFF_GLU:OFF_GLU + D_CONV]
        a_gate = z[..., OFF_GLU + D_CONV:OFF_GATE]
        gates = jax.nn.sigmoid(z[..., OFF_GATE:]).reshape(B, S, N_BRANCH, D)

        y_a = spatial_gating(u, v, sgu_ln_g[l], sgu_ln_b[l], sgu_w_s[l], sgu_b_s[l]) @ w_pa[l]
        y_b = multiscale_pool(p, pool_w[l], pool_scale[l]) @ w_pb[l]
        y_c = conformer_conv(a, a_gate, conv_w[l], conv_b[l], conv_ln_g[l], conv_ln_b[l]) @ w_pc[l]
        merged = gates[:, :, 0] * y_a + gates[:, :, 1] * y_b + gates[:, :, 2] * y_c
        x = x + gt_m * (merged @ w_out[l])

        h = rmsnorm(x, g_ffn[l]) * (1.0 + sc_f) + sh_f
        gu = h @ w_ffn_in[l]
        g_part, u_part = jnp.split(gu, 2, axis=-1)
        x = x + gt_f * ((jax.nn.silu(g_part) * u_part) @ w_ffn_out[l])
    return rmsnorm(x, g_final)
```

```python
import functools

import jax
import jax.numpy as jnp
from jax import lax
from jax.experimental import pallas as pl
from jax.experimental.pallas import tpu as pltpu

D_MODEL = 1024
DEPTH = 2
CHUNK = 128
SGU_GROUPS = 8
SGU_GC = D_MODEL // SGU_GROUPS
POOL_WINDOWS = (2, 4, 8, 16)
POOL_GROUPS = len(POOL_WINDOWS)
POOL_GC = D_MODEL // POOL_GROUPS
CONV_WIDTH = 31
D_FF = 2816
EPS = 1e-6

OFF_U = 0
OFF_V = D_MODEL
OFF_POOL = 2 * D_MODEL
OFF_GLU_A = 3 * D_MODEL
OFF_GLU_G = 4 * D_MODEL
OFF_GATE = 5 * D_MODEL
D_IN = 8 * D_MODEL

HALO = 32
TM_MIX = 256
TM_FFN = 256
VMEM_LIMIT_BYTES = 56 * 1024 * 1024

ADA_SH_M, ADA_SC_M, ADA_GT_M, ADA_SH_F, ADA_SC_F, ADA_GT_F = range(6)


def _bdot(a, b):
    return jnp.dot(a, b, preferred_element_type=jnp.float32)


def _sigmoid(x):
    return 1.0 / (1.0 + jnp.exp(-x))


def _gelu(x):
    return 0.5 * x * (1.0 + lax.erf(x * (2.0 ** -0.5)))


def _layernorm(x, g, b):
    mu = jnp.mean(x, axis=-1, keepdims=True)
    xc = x - mu
    var = jnp.mean(xc * xc, axis=-1, keepdims=True)
    return xc * lax.rsqrt(var + EPS) * g + b


def _modulated_rmsnorm(x, g, scale, shift):
    y = x * lax.rsqrt(jnp.mean(x * x, axis=-1, keepdims=True) + EPS) * g
    return y * (1.0 + scale) + shift


def _ada_kernel(c_ref, w_ref, b_ref, o_ref):
    c = c_ref[...]
    c_act = c * _sigmoid(c)
    o_ref[...] = jnp.dot(c_act, w_ref[...], preferred_element_type=jnp.float32,
                         precision=lax.Precision.HIGHEST) + b_ref[...]


def _ada_table(c, w_ada, b_ada):
    L, D, D6 = w_ada.shape
    B = c.shape[0]
    nblk = D6 // D
    return pl.pallas_call(
        _ada_kernel,
        grid=(L, nblk),
        in_specs=[
            pl.BlockSpec((B, D), lambda l, j: (0, 0)),
            pl.BlockSpec((None, D, D), lambda l, j: (l, 0, j)),
            pl.BlockSpec((None, 1, D), lambda l, j: (l, 0, j)),
        ],
        out_specs=pl.BlockSpec((None, B, D), lambda l, j: (l, 0, j)),
        out_shape=jax.ShapeDtypeStruct((L, B, D6), jnp.float32),
        compiler_params=pltpu.CompilerParams(dimension_semantics=("arbitrary", "arbitrary")),
        name="ada_table",
    )(c, w_ada, b_ada.reshape(L, 1, D6))


def _mixer_kernel(x_ref, ada_ref, gmix_ref, win_ref, bin_ref, lng_ref, lnb_ref, ws_ref, bs_ref,
                  wpa_ref, poolw_ref, pscale_ref, wpb_ref, convw_ref, convb_ref, clng_ref,
                  clnb_ref, wpc_ref, wout_ref, o_ref, wm_scr, mix_scr, pbuf, zbuf):
    tm = x_ref.shape[0]
    b_id = pl.program_id(0)
    s_id = pl.program_id(1)

    @pl.when(jnp.logical_and(b_id == 0, s_id == 0))
    def _():
        row = lax.broadcasted_iota(jnp.int32, (CHUNK, CHUNK), 0)
        col = lax.broadcasted_iota(jnp.int32, (CHUNK, CHUNK), 1)
        for g in range(SGU_GROUPS):
            wm_scr[g] = jnp.where(row >= col, ws_ref[g], 0.0).astype(jnp.bfloat16)

    @pl.when(s_id == 0)
    def _():
        pbuf[0:HALO, :] = jnp.zeros((HALO, D_MODEL), jnp.float32)
        zbuf[0:HALO, :] = jnp.zeros((HALO, D_MODEL), jnp.float32)

    x = x_ref[...]
    h = _modulated_rmsnorm(x, gmix_ref[...], ada_ref[ADA_SC_M:ADA_SC_M + 1, :],
                           ada_ref[ADA_SH_M:ADA_SH_M + 1, :]).astype(jnp.bfloat16)

    def proj(off):
        return _bdot(h, win_ref[:, off:off + D_MODEL]) + bin_ref[:, off:off + D_MODEL]

    def gate(i):
        return _sigmoid(proj(OFF_GATE + i * D_MODEL))

    v = _layernorm(_gelu(proj(OFF_V)), lng_ref[...], lnb_ref[...]).astype(jnp.bfloat16)
    n_chunks = tm // CHUNK
    for g in range(SGU_GROUPS):
        cols = slice(g * SGU_GC, (g + 1) * SGU_GC)
        rhs = jnp.concatenate([v[n * CHUNK:(n + 1) * CHUNK, cols] for n in range(n_chunks)], axis=1)
        mixed = _bdot(wm_scr[g], rhs) + bs_ref[:, g:g + 1]
        for n in range(n_chunks):
            mix_scr[n * CHUNK:(n + 1) * CHUNK, cols] = mixed[:, n * SGU_GC:(n + 1) * SGU_GC]
    u = _gelu(proj(OFF_U))
    s_a = (u * mix_scr[...]).astype(jnp.bfloat16)
    acc = gate(0) * _bdot(s_a, wpa_ref[...])

    p = proj(OFF_POOL)
    pbuf[HALO:HALO + tm, :] = p
    pos = s_id * tm + lax.broadcasted_iota(jnp.int32, (tm, 1), 0) + 1
    ys = []
    for i, win in enumerate(POOL_WINDOWS):
        cols = slice(i * POOL_GC, (i + 1) * POOL_GC)
        wsum = p[:, cols]
        for j in range(1, win):
            wsum = wsum + pbuf[HALO - j:HALO - j + tm, cols]
        inv_cnt = 1.0 / jnp.minimum(pos, win).astype(jnp.float32)
        pooled = (wsum * inv_cnt - p[:, cols]).astype(jnp.bfloat16)
        ys.append(_bdot(pooled, poolw_ref[i]))
    y_pool = (jnp.concatenate(ys, axis=1) * pscale_ref[...]).astype(jnp.bfloat16)
    acc = acc + gate(1) * _bdot(y_pool, wpb_ref[...])
    pbuf[0:HALO, :] = pbuf[tm:tm + HALO, :]

    z = proj(OFF_GLU_A) * _sigmoid(proj(OFF_GLU_G))
    zbuf[HALO:HALO + tm, :] = z
    conv = jnp.zeros((tm, D_MODEL), jnp.float32) + convb_ref[...]
    for k in range(CONV_WIDTH):
        off = HALO - (CONV_WIDTH - 1) + k
        conv = conv + zbuf[off:off + tm, :] * convw_ref[k:k + 1, :]
    zn = _layernorm(conv, clng_ref[...], clnb_ref[...])
    y_conv = (zn * _sigmoid(zn)).astype(jnp.bfloat16)
    acc = acc + gate(2) * _bdot(y_conv, wpc_ref[...])
    zbuf[0:HALO, :] = zbuf[tm:tm + HALO, :]

    o_ref[...] = x + ada_ref[ADA_GT_M:ADA_GT_M + 1, :] * _bdot(acc.astype(jnp.bfloat16), wout_ref[...])


def _resident(shape):
    nd = len(shape)
    return pl.BlockSpec(shape, lambda b, s: (0,) * nd, pipeline_mode=pl.Buffered(1))


def _mixer_layer(x, ada, g_mix, w_in, b_in, ln_g, ln_b, w_s, b_s_col, w_pa, pool_w, pool_scale,
                 w_pb, conv_w, conv_b, cln_g, cln_b, w_pc, w_out):
    B, S, D = x.shape
    tm = TM_MIX
    tile = pl.BlockSpec((None, tm, D), lambda b, s: (b, s, 0))
    operands = (g_mix, w_in, b_in, ln_g, ln_b, w_s, b_s_col, w_pa, pool_w, pool_scale, w_pb,
                conv_w, conv_b, cln_g, cln_b, w_pc, w_out)
    return pl.pallas_call(
        _mixer_kernel,
        grid=(B, S // tm),
        in_specs=[tile, pl.BlockSpec((None, 6, D), lambda b, s: (b, 0, 0))]
                 + [_resident(a.shape) for a in operands],
        out_specs=tile,
        out_shape=jax.ShapeDtypeStruct(x.shape, x.dtype),
        scratch_shapes=[
            pltpu.VMEM((SGU_GROUPS, CHUNK, CHUNK), jnp.bfloat16),
            pltpu.VMEM((tm, D), jnp.float32),
            pltpu.VMEM((HALO + tm, D), jnp.float32),
            pltpu.VMEM((HALO + tm, D), jnp.float32),
        ],
        compiler_params=pltpu.CompilerParams(
            dimension_semantics=("arbitrary", "arbitrary"), vmem_limit_bytes=VMEM_LIMIT_BYTES),
        name="mixer_layer",
    )(x, ada, *operands)


def _ffn_kernel(x_ref, ada_ref, g_ref, win_ref, wout_ref, gfin_ref, o_ref, *, final_norm):
    x = x_ref[...]
    h = _modulated_rmsnorm(x, g_ref[...], ada_ref[ADA_SC_F:ADA_SC_F + 1, :],
                           ada_ref[ADA_SH_F:ADA_SH_F + 1, :]).astype(jnp.bfloat16)
    g_part = _bdot(h, win_ref[:, 0:D_FF])
    u_part = _bdot(h, win_ref[:, D_FF:2 * D_FF])
    act = (g_part * _sigmoid(g_part) * u_part).astype(jnp.bfloat16)
    y = x + ada_ref[ADA_GT_F:ADA_GT_F + 1, :] * _bdot(act, wout_ref[...])
    if final_norm:
        y = y * lax.rsqrt(jnp.mean(y * y, axis=-1, keepdims=True) + EPS) * gfin_ref[...]
    o_ref[...] = y


def _ffn_layer(x, ada, g_ffn, w_ffn_in, w_ffn_out, g_final, final_norm):
    B, S, D = x.shape
    tm = TM_FFN
    tile = pl.BlockSpec((None, tm, D), lambda b, s: (b, s, 0))
    operands = (g_ffn, w_ffn_in, w_ffn_out, g_final)
    return pl.pallas_call(
        functools.partial(_ffn_kernel, final_norm=final_norm),
        grid=(B, S // tm),
        in_specs=[tile, pl.BlockSpec((None, 6, D), lambda b, s: (b, 0, 0))]
                 + [_resident(a.shape) for a in operands],
        out_specs=tile,
        out_shape=jax.ShapeDtypeStruct(x.shape, x.dtype),
        compiler_params=pltpu.CompilerParams(
            dimension_semantics=("arbitrary", "arbitrary"), vmem_limit_bytes=VMEM_LIMIT_BYTES),
        name="ffn_layer",
    )(x, ada, *operands)


def kernel(x, c, w_ada, b_ada, g_mix, w_in, b_in, sgu_ln_g, sgu_ln_b, sgu_w_s, sgu_b_s, w_pa, pool_w, pool_scale, w_pb, conv_w, conv_b, conv_ln_g, conv_ln_b, w_pc, w_out, g_ffn, w_ffn_in, w_ffn_out, g_final):
    B, S, D = x.shape
    assert D == D_MODEL and S % TM_MIX == 0 and S % TM_FFN == 0 and TM_MIX % CHUNK == 0
    bf = jnp.bfloat16
    row = lambda a: a.reshape(1, -1)
    ada = _ada_table(c, w_ada, b_ada).reshape(DEPTH, B, 6, D)
    for l in range(DEPTH):
        x = _mixer_layer(
            x, ada[l], row(g_mix[l]), w_in[l].astype(bf), row(b_in[l]), row(sgu_ln_g[l]),
            row(sgu_ln_b[l]), sgu_w_s[l], jnp.transpose(sgu_b_s[l]), w_pa[l].astype(bf),
            pool_w[l].astype(bf), row(pool_scale[l]), w_pb[l].astype(bf), conv_w[l],
            row(conv_b[l]), row(conv_ln_g[l]), row(conv_ln_b[l]), w_pc[l].astype(bf),
            w_out[l].astype(bf))
        x = _ffn_layer(x, ada[l], row(g_ffn[l]), w_ffn_in[l].astype(bf), w_ffn_out[l].astype(bf),
                       row(g_final), final_norm=(l == DEPTH - 1))
    return x
```

```python
import functools

import jax
import jax.numpy as jnp
from jax import lax
from jax.experimental import pallas as pl
from jax.experimental.pallas import tpu as pltpu

D_MODEL = 1024
DEPTH = 2
CHUNK = 128
SGU_GROUPS = 8
SGU_GC = D_MODEL // SGU_GROUPS
POOL_WINDOWS = (2, 4, 8, 16)
POOL_GROUPS = len(POOL_WINDOWS)
POOL_GC = D_MODEL // POOL_GROUPS
CONV_WIDTH = 31
D_FF = 2816
EPS = 1e-6

OFF_U = 0
OFF_V = D_MODEL
OFF_POOL = 2 * D_MODEL
OFF_GLU_A = 3 * D_MODEL
OFF_GLU_G = 4 * D_MODEL
OFF_GATE = 5 * D_MODEL
D_IN = 8 * D_MODEL

SUBLANES = 8
HALO = 32
SHIFT_ROWS_EXTRA = HALO - SUBLANES
TM_MIX = 256
TM_FFN = 256
CONV_ROWS = 64
CONV_COLS = 256
VMEM_LIMIT_BYTES = 56 * 1024 * 1024

ADA_SH_M, ADA_SC_M, ADA_GT_M, ADA_SH_F, ADA_SC_F, ADA_GT_F = range(6)


def _pack_bf16_rows(w):
    k, n = w.shape[-2:]
    wb = w.astype(jnp.bfloat16).reshape(*w.shape[:-2], k // 2, 2, n)
    return lax.bitcast_convert_type(jnp.swapaxes(wb, -1, -2), jnp.uint32)


def _wdot(a, w_packed):
    return jnp.dot(a, pltpu.bitcast(w_packed, jnp.bfloat16), preferred_element_type=jnp.float32)


def _sigmoid(x):
    return 0.5 * jnp.tanh(0.5 * x) + 0.5


def _silu(x):
    hx = 0.5 * x
    return hx * jnp.tanh(hx) + hx


def _gelu(x):
    hx = 0.5 * x
    return hx * lax.erf(x * (2.0 ** -0.5)) + hx


def _layernorm(x, g, b):
    mu = jnp.mean(x, axis=-1, keepdims=True)
    xc = x - mu
    var = jnp.mean(xc * xc, axis=-1, keepdims=True)
    return xc * lax.rsqrt(var + EPS) * g + b


def _modulated_rmsnorm(x, g, scale, shift):
    y = x * lax.rsqrt(jnp.mean(x * x, axis=-1, keepdims=True) + EPS) * g
    return y * (1.0 + scale) + shift


def _ada_kernel(c_ref, w_ref, b_ref, o_ref):
    c = c_ref[...]
    c_act = c / (1.0 + jnp.exp(-c))
    o_ref[...] = jnp.dot(c_act, w_ref[...], preferred_element_type=jnp.float32,
                         precision=lax.Precision.HIGHEST) + b_ref[...]


def _ada_table(c, w_ada, b_ada):
    L, D, D6 = w_ada.shape
    B = c.shape[0]
    nblk = D6 // D
    return pl.pallas_call(
        _ada_kernel,
        grid=(L, nblk),
        in_specs=[
            pl.BlockSpec((B, D), lambda l, j: (0, 0)),
            pl.BlockSpec((None, D, D), lambda l, j: (l, 0, j)),
            pl.BlockSpec((None, 1, D), lambda l, j: (l, 0, j)),
        ],
        out_specs=pl.BlockSpec((None, B, D), lambda l, j: (l, 0, j)),
        out_shape=jax.ShapeDtypeStruct((L, B, D6), jnp.float32),
        compiler_params=pltpu.CompilerParams(dimension_semantics=("arbitrary", "arbitrary")),
        name="ada_table",
    )(c, w_ada, b_ada.reshape(L, 1, D6))


def _mixer_kernel(x_ref, ada_ref, gmix_ref, win_ref, bin_ref, lng_ref, lnb_ref, ws_ref, bs_ref,
                  wpa_ref, poolw_ref, pscale_ref, wpb_ref, convw_ref, convb_ref, clng_ref,
                  clnb_ref, wpc_ref, wout_ref, o_ref,
                  wm_scr, h_scr, mix_scr, pbuf, zbuf, zsh, conv_scr, acc_scr):
    tm = x_ref.shape[0]
    b_id = pl.program_id(0)
    s_id = pl.program_id(1)

    @pl.when(jnp.logical_and(b_id == 0, s_id == 0))
    def _():
        row = lax.broadcasted_iota(jnp.int32, (CHUNK, CHUNK), 0)
        col = lax.broadcasted_iota(jnp.int32, (CHUNK, CHUNK), 1)
        for g in range(SGU_GROUPS):
            wm_scr[g] = jnp.where(row >= col, ws_ref[g], 0.0).astype(jnp.bfloat16)

    @pl.when(s_id == 0)
    def _():
        pbuf[0:HALO, :] = jnp.zeros((HALO, D_MODEL), jnp.float32)
        zbuf[0:HALO, :] = jnp.zeros((HALO, D_MODEL), jnp.float32)

    h_scr[...] = _modulated_rmsnorm(
        x_ref[...], gmix_ref[...], ada_ref[ADA_SC_M:ADA_SC_M + 1, :],
        ada_ref[ADA_SH_M:ADA_SH_M + 1, :]).astype(jnp.bfloat16)

    def proj(off):
        return _wdot(h_scr[...], win_ref[:, off:off + D_MODEL]) + bin_ref[:, off:off + D_MODEL]

    def gate(i):
        return _sigmoid(proj(OFF_GATE + i * D_MODEL))

    def conv_block(j):
        cols = slice(j * CONV_COLS, (j + 1) * CONV_COLS)
        for r0 in range(0, tm, CONV_ROWS):
            acc = jnp.zeros((CONV_ROWS, CONV_COLS), jnp.float32) + convb_ref[:, cols]
            for k in range(CONV_WIDTH):
                off = HALO - (CONV_WIDTH - 1) + k
                shift = off % SUBLANES
                base = off - shift + r0
                if shift == 0:
                    src = zbuf[base:base + CONV_ROWS, cols]
                else:
                    src = zsh[shift - 1, base:base + CONV_ROWS, cols]
                acc = acc + src * convw_ref[k:k + 1, cols]
            conv_scr[r0:r0 + CONV_ROWS, cols] = acc

    n_conv_blocks = D_MODEL // CONV_COLS

    zbuf[HALO:HALO + tm, :] = proj(OFF_GLU_A) * _sigmoid(proj(OFF_GLU_G))
    for shift in range(1, SUBLANES):
        zsh[shift - 1] = zbuf[shift:shift + tm + SHIFT_ROWS_EXTRA, :]

    v = _layernorm(_gelu(proj(OFF_V)), lng_ref[...], lnb_ref[...]).astype(jnp.bfloat16)
    conv_block(0)
    n_chunks = tm // CHUNK
    for g in range(SGU_GROUPS):
        cols = slice(g * SGU_GC, (g + 1) * SGU_GC)
        rhs = jnp.concatenate([v[n * CHUNK:(n + 1) * CHUNK, cols] for n in range(n_chunks)], axis=1)
        mixed = jnp.dot(wm_scr[g], rhs, preferred_element_type=jnp.float32) + bs_ref[:, g:g + 1]
        for n in range(n_chunks):
            mix_scr[n * CHUNK:(n + 1) * CHUNK, cols] = mixed[:, n * SGU_GC:(n + 1) * SGU_GC]
    s_a = (_gelu(proj(OFF_U)) * mix_scr[...]).astype(jnp.bfloat16)
    conv_block(1)
    acc_scr[...] = gate(0) * _wdot(s_a, wpa_ref[...])

    pbuf[HALO:HALO + tm, :] = proj(OFF_POOL)
    conv_block(2)
    pos = s_id * tm + lax.broadcasted_iota(jnp.int32, (tm, 1), 0) + 1
    ys = []
    for i, win in enumerate(POOL_WINDOWS):
        cols = slice(i * POOL_GC, (i + 1) * POOL_GC)
        p = pbuf[HALO:HALO + tm, cols]
        wsum = p
        for j in range(1, win):
            wsum = wsum + pbuf[HALO - j:HALO - j + tm, cols]
        inv_cnt = 1.0 / jnp.minimum(pos, win).astype(jnp.float32)
        pooled = (wsum * inv_cnt - p).astype(jnp.bfloat16)
        ys.append(_wdot(pooled, poolw_ref[i]))
    y_pool = (jnp.concatenate(ys, axis=1) * pscale_ref[...]).astype(jnp.bfloat16)
    pbuf[0:HALO, :] = pbuf[tm:tm + HALO, :]
    for j in range(3, n_conv_blocks):
        conv_block(j)
    acc_scr[...] = acc_scr[...] + gate(1) * _wdot(y_pool, wpb_ref[...])

    zbuf[0:HALO, :] = zbuf[tm:tm + HALO, :]
    y_conv = _silu(_layernorm(conv_scr[...], clng_ref[...], clnb_ref[...])).astype(jnp.bfloat16)
    merged = (acc_scr[...] + gate(2) * _wdot(y_conv, wpc_ref[...])).astype(jnp.bfloat16)

    o_ref[...] = x_ref[...] + ada_ref[ADA_GT_M:ADA_GT_M + 1, :] * _wdot(merged, wout_ref[...])


def _resident(shape):
    nd = len(shape)
    return pl.BlockSpec(shape, lambda b, s: (0,) * nd, pipeline_mode=pl.Buffered(1))


def _mixer_operands(l, g_mix, w_in, b_in, sgu_ln_g, sgu_ln_b, sgu_w_s, sgu_b_s, w_pa, pool_w,
                    pool_scale, w_pb, conv_w, conv_b, conv_ln_g, conv_ln_b, w_pc, w_out):
    row = lambda a: a.reshape(1, -1)
    return (row(g_mix[l]), _pack_bf16_rows(w_in[l]), row(b_in[l]), row(sgu_ln_g[l]),
            row(sgu_ln_b[l]), sgu_w_s[l], jnp.transpose(sgu_b_s[l]), _pack_bf16_rows(w_pa[l]),
            _pack_bf16_rows(pool_w[l]), row(pool_scale[l]), _pack_bf16_rows(w_pb[l]), conv_w[l],
            row(conv_b[l]), row(conv_ln_g[l]), row(conv_ln_b[l]), _pack_bf16_rows(w_pc[l]),
            _pack_bf16_rows(w_out[l]))


def _mixer_layer(x, ada, operands):
    B, S, D = x.shape
    tm = TM_MIX
    tile = pl.BlockSpec((None, tm, D), lambda b, s: (b, s, 0))
    return pl.pallas_call(
        _mixer_kernel,
        grid=(B, S // tm),
        in_specs=[tile, pl.BlockSpec((None, 6, D), lambda b, s: (b, 0, 0))]
                 + [_resident(a.shape) for a in operands],
        out_specs=tile,
        out_shape=jax.ShapeDtypeStruct(x.shape, x.dtype),
        scratch_shapes=[
            pltpu.VMEM((SGU_GROUPS, CHUNK, CHUNK), jnp.bfloat16),
            pltpu.VMEM((tm, D), jnp.bfloat16),
            pltpu.VMEM((tm, D), jnp.float32),
            pltpu.VMEM((HALO + tm, D), jnp.float32),
            pltpu.VMEM((HALO + tm, D), jnp.float32),
            pltpu.VMEM((SUBLANES - 1, tm + SHIFT_ROWS_EXTRA, D), jnp.float32),
            pltpu.VMEM((tm, D), jnp.float32),
            pltpu.VMEM((tm, D), jnp.float32),
        ],
        compiler_params=pltpu.CompilerParams(
            dimension_semantics=("arbitrary", "arbitrary"), vmem_limit_bytes=VMEM_LIMIT_BYTES),
        name="mixer_layer",
    )(x, ada, *operands)


def _ffn_kernel(x_ref, ada_ref, g_ref, win_ref, wout_ref, gfin_ref, o_ref, *, final_norm):
    x = x_ref[...]
    h = _modulated_rmsnorm(x, g_ref[...], ada_ref[ADA_SC_F:ADA_SC_F + 1, :],
                           ada_ref[ADA_SH_F:ADA_SH_F + 1, :]).astype(jnp.bfloat16)
    g_part = _wdot(h, win_ref[:, 0:D_FF])
    u_part = _wdot(h, win_ref[:, D_FF:2 * D_FF])
    act = (_silu(g_part) * u_part).astype(jnp.bfloat16)
    y = x + ada_ref[ADA_GT_F:ADA_GT_F + 1, :] * _wdot(act, wout_ref[...])
    if final_norm:
        y = y * lax.rsqrt(jnp.mean(y * y, axis=-1, keepdims=True) + EPS) * gfin_ref[...]
    o_ref[...] = y


def _ffn_operands(l, g_ffn, w_ffn_in, w_ffn_out, g_final):
    return (g_ffn[l].reshape(1, -1), _pack_bf16_rows(w_ffn_in[l]), _pack_bf16_rows(w_ffn_out[l]),
            g_final.reshape(1, -1))


def _ffn_layer(x, ada, operands, final_norm):
    B, S, D = x.shape
    tm = TM_FFN
    tile = pl.BlockSpec((None, tm, D), lambda b, s: (b, s, 0))
    return pl.pallas_call(
        functools.partial(_ffn_kernel, final_norm=final_norm),
        grid=(B, S // tm),
        in_specs=[tile, pl.BlockSpec((None, 6, D), lambda b, s: (b, 0, 0))]
                 + [_resident(a.shape) for a in operands],
        out_specs=tile,
        out_shape=jax.ShapeDtypeStruct(x.shape, x.dtype),
        compiler_params=pltpu.CompilerParams(
            dimension_semantics=("arbitrary", "arbitrary"), vmem_limit_bytes=VMEM_LIMIT_BYTES),
        name="ffn_layer",
    )(x, ada, *operands)


def kernel(x, c, w_ada, b_ada, g_mix, w_in, b_in, sgu_ln_g, sgu_ln_b, sgu_w_s, sgu_b_s, w_pa, pool_w, pool_scale, w_pb, conv_w, conv_b, conv_ln_g, conv_ln_b, w_pc, w_out, g_ffn, w_ffn_in, w_ffn_out, g_final):
    B, S, D = x.shape
    assert D == D_MODEL and S % TM_MIX == 0 and S % TM_FFN == 0 and TM_MIX % CHUNK == 0
    ada = _ada_table(c, w_ada, b_ada).reshape(DEPTH, B, 6, D)
    for l in range(DEPTH):
        x = _mixer_layer(x, ada[l], _mixer_operands(
            l, g_mix, w_in, b_in, sgu_ln_g, sgu_ln_b, sgu_w_s, sgu_b_s, w_pa, pool_w, pool_scale,
            w_pb, conv_w, conv_b, conv_ln_g, conv_ln_b, w_pc, w_out))
        x = _ffn_layer(x, ada[l], _ffn_operands(l, g_ffn, w_ffn_in, w_ffn_out, g_final),
                       final_norm=(l == DEPTH - 1))
    return x
```

```python
import functools

import jax
import jax.numpy as jnp
from jax import lax
from jax.experimental import pallas as pl
from jax.experimental.pallas import tpu as pltpu

D_MODEL = 1024
DEPTH = 2
CHUNK = 128
SGU_GROUPS = 8
SGU_GC = D_MODEL // SGU_GROUPS
POOL_WINDOWS = (2, 4, 8, 16)
POOL_GROUPS = len(POOL_WINDOWS)
POOL_GC = D_MODEL // POOL_GROUPS
CONV_WIDTH = 31
D_FF = 2816
EPS = 1e-6

OFF_U = 0
OFF_V = D_MODEL
OFF_POOL = 2 * D_MODEL
OFF_GLU_A = 3 * D_MODEL
OFF_GLU_G = 4 * D_MODEL
OFF_GATE = 5 * D_MODEL
D_IN = 8 * D_MODEL

SUBLANES = 8
LANES = 128
BF16_ROWS = 16
PACK_BLOCK_ROWS = 512
PACK_BLOCK_COLS = 1024
HALO = 32
SHIFT_ROWS_EXTRA = HALO - SUBLANES
TM_MIX = 256
TM_FFN = 512
CONV_ROWS = 64
CONV_COLS = 256
VMEM_LIMIT_BYTES = 56 * 1024 * 1024

ADA_SH_M, ADA_SC_M, ADA_GT_M, ADA_SH_F, ADA_SC_F, ADA_GT_F = range(6)


def _pack_kernel(w_ref, o_ref):
    o_ref[...] = pltpu.bitcast(w_ref[...].astype(jnp.bfloat16), jnp.uint32)


def _largest_divisor(n, multiple_of, at_most):
    return max(d for d in range(multiple_of, at_most + 1, multiple_of) if n % d == 0)


def _pack_bf16_rows(w):
    L, K, N = w.shape
    kb = _largest_divisor(K, 2 * BF16_ROWS, PACK_BLOCK_ROWS)
    nb = _largest_divisor(N, LANES, PACK_BLOCK_COLS)
    return pl.pallas_call(
        _pack_kernel,
        grid=(L, K // kb, N // nb),
        in_specs=[pl.BlockSpec((None, kb, nb), lambda l, i, j: (l, i, j))],
        out_specs=pl.BlockSpec((None, kb // 2, nb), lambda l, i, j: (l, i, j)),
        out_shape=jax.ShapeDtypeStruct((L, K // 2, N), jnp.uint32),
        compiler_params=pltpu.CompilerParams(
            dimension_semantics=("arbitrary", "arbitrary", "arbitrary")),
        name="pack_bf16_rows",
    )(w)


def _wdot(a, w_packed):
    return jnp.dot(a, pltpu.bitcast(w_packed, jnp.bfloat16), preferred_element_type=jnp.float32)


def _sigmoid(x):
    return 0.5 * jnp.tanh(0.5 * x) + 0.5


def _silu(x):
    hx = 0.5 * x
    return hx * jnp.tanh(hx) + hx


def _gelu(x):
    hx = 0.5 * x
    return hx * lax.erf(x * (2.0 ** -0.5)) + hx


def _layernorm(x, g, b):
    mu = jnp.mean(x, axis=-1, keepdims=True)
    xc = x - mu
    var = jnp.mean(xc * xc, axis=-1, keepdims=True)
    return xc * lax.rsqrt(var + EPS) * g + b


def _modulated_rmsnorm(x, g, scale, shift):
    y = x * lax.rsqrt(jnp.mean(x * x, axis=-1, keepdims=True) + EPS) * g
    return y * (1.0 + scale) + shift


def _ada_kernel(c_ref, w_ref, b_ref, o_ref):
    c = c_ref[...]
    c_act = c / (1.0 + jnp.exp(-c))
    o_ref[...] = jnp.dot(c_act, w_ref[...], preferred_element_type=jnp.float32,
                         precision=lax.Precision.HIGHEST) + b_ref[...]


def _ada_table(c, w_ada, b_ada):
    L, D, D6 = w_ada.shape
    B = c.shape[0]
    nblk = D6 // D
    return pl.pallas_call(
        _ada_kernel,
        grid=(L, nblk),
        in_specs=[
            pl.BlockSpec((B, D), lambda l, j: (0, 0)),
            pl.BlockSpec((None, D, D), lambda l, j: (l, 0, j)),
            pl.BlockSpec((None, 1, D), lambda l, j: (l, 0, j)),
        ],
        out_specs=pl.BlockSpec((None, B, D), lambda l, j: (l, 0, j)),
        out_shape=jax.ShapeDtypeStruct((L, B, D6), jnp.float32),
        compiler_params=pltpu.CompilerParams(dimension_semantics=("arbitrary", "arbitrary")),
        name="ada_table",
    )(c, w_ada, b_ada.reshape(L, 1, D6))


def _mixer_kernel(x_ref, ada_ref, gmix_ref, win_ref, bin_ref, lng_ref, lnb_ref, ws_ref, bs_ref,
                  wpa_ref, poolw_ref, pscale_ref, wpb_ref, convw_ref, convb_ref, clng_ref,
                  clnb_ref, wpc_ref, wout_ref, o_ref,
                  wm_scr, h_scr, mix_scr, pbuf, pool_lvl, zbuf, zsh, conv_scr, acc_scr):
    tm = x_ref.shape[0]
    b_id = pl.program_id(0)
    s_id = pl.program_id(1)

    @pl.when(jnp.logical_and(b_id == 0, s_id == 0))
    def _():
        row = lax.broadcasted_iota(jnp.int32, (CHUNK, CHUNK), 0)
        col = lax.broadcasted_iota(jnp.int32, (CHUNK, CHUNK), 1)
        for g in range(SGU_GROUPS):
            wm_scr[g] = jnp.where(row >= col, ws_ref[g], 0.0).astype(jnp.bfloat16)

    @pl.when(s_id == 0)
    def _():
        pbuf[0:HALO, :] = jnp.zeros((HALO, D_MODEL), jnp.float32)
        zbuf[0:HALO, :] = jnp.zeros((HALO, D_MODEL), jnp.float32)

    h_scr[...] = _modulated_rmsnorm(
        x_ref[...], gmix_ref[...], ada_ref[ADA_SC_M:ADA_SC_M + 1, :],
        ada_ref[ADA_SH_M:ADA_SH_M + 1, :]).astype(jnp.bfloat16)

    def proj(off):
        return _wdot(h_scr[...], win_ref[:, off:off + D_MODEL]) + bin_ref[:, off:off + D_MODEL]

    def gate(i):
        return _sigmoid(proj(OFF_GATE + i * D_MODEL))

    def conv_block(j):
        cols = slice(j * CONV_COLS, (j + 1) * CONV_COLS)
        for r0 in range(0, tm, CONV_ROWS):
            acc = jnp.zeros((CONV_ROWS, CONV_COLS), jnp.float32) + convb_ref[:, cols]
            for k in range(CONV_WIDTH):
                off = HALO - (CONV_WIDTH - 1) + k
                shift = off % SUBLANES
                base = off - shift + r0
                if shift == 0:
                    src = zbuf[base:base + CONV_ROWS, cols]
                else:
                    src = zsh[shift - 1, base:base + CONV_ROWS, cols]
                acc = acc + src * convw_ref[k:k + 1, cols]
            conv_scr[r0:r0 + CONV_ROWS, cols] = acc

    def window_sum(cols, win):
        first = {win: 0}
        k = win // 2
        while k >= 1:
            first[k] = -(-(k - first[2 * k]) // SUBLANES) * -SUBLANES
            k //= 2
        src, src_first, k = pbuf, -HALO, 1
        src_cols = cols
        while True:
            lo = first[2 * k] - src_first
            n = tm - first[2 * k]
            s = src[lo:lo + n, src_cols] + src[lo - k:lo - k + n, src_cols]
            if 2 * k == win:
                return s
            slot = pool_lvl.at[(2 * k).bit_length() % 2]
            slot[0:n, :] = s
            src, src_first, src_cols, k = slot, first[2 * k], slice(None), 2 * k

    n_conv_blocks = D_MODEL // CONV_COLS

    zbuf[HALO:HALO + tm, :] = proj(OFF_GLU_A) * _sigmoid(proj(OFF_GLU_G))
    for shift in range(1, SUBLANES):
        zsh[shift - 1] = zbuf[shift:shift + tm + SHIFT_ROWS_EXTRA, :]

    v = _layernorm(_gelu(proj(OFF_V)), lng_ref[...], lnb_ref[...]).astype(jnp.bfloat16)
    conv_block(0)
    n_chunks = tm // CHUNK
    for g in range(SGU_GROUPS):
        cols = slice(g * SGU_GC, (g + 1) * SGU_GC)
        rhs = jnp.concatenate([v[n * CHUNK:(n + 1) * CHUNK, cols] for n in range(n_chunks)], axis=1)
        mixed = jnp.dot(wm_scr[g], rhs, preferred_element_type=jnp.float32) + bs_ref[:, g:g + 1]
        for n in range(n_chunks):
            mix_scr[n * CHUNK:(n + 1) * CHUNK, cols] = mixed[:, n * SGU_GC:(n + 1) * SGU_GC]
    s_a = (_gelu(proj(OFF_U)) * mix_scr[...]).astype(jnp.bfloat16)
    conv_block(1)
    acc_scr[...] = gate(0) * _wdot(s_a, wpa_ref[...])

    pbuf[HALO:HALO + tm, :] = proj(OFF_POOL)
    conv_block(2)
    pos = s_id * tm + lax.broadcasted_iota(jnp.int32, (tm, 1), 0) + 1
    ys = []
    for i, win in enumerate(POOL_WINDOWS):
        cols = slice(i * POOL_GC, (i + 1) * POOL_GC)
        p = pbuf[HALO:HALO + tm, cols]
        wsum = window_sum(cols, win)
        inv_cnt = 1.0 / jnp.minimum(pos, win).astype(jnp.float32)
        pooled = (wsum * inv_cnt - p).astype(jnp.bfloat16)
        ys.append(_wdot(pooled, poolw_ref[i]))
    y_pool = (jnp.concatenate(ys, axis=1) * pscale_ref[...]).astype(jnp.bfloat16)
    pbuf[0:HALO, :] = pbuf[tm:tm + HALO, :]
    for j in range(3, n_conv_blocks):
        conv_block(j)
    acc_scr[...] = acc_scr[...] + gate(1) * _wdot(y_pool, wpb_ref[...])

    zbuf[0:HALO, :] = zbuf[tm:tm + HALO, :]
    y_conv = _silu(_layernorm(conv_scr[...], clng_ref[...], clnb_ref[...])).astype(jnp.bfloat16)
    merged = (acc_scr[...] + gate(2) * _wdot(y_conv, wpc_ref[...])).astype(jnp.bfloat16)

    o_ref[...] = x_ref[...] + ada_ref[ADA_GT_M:ADA_GT_M + 1, :] * _wdot(merged, wout_ref[...])


def _resident(shape):
    nd = len(shape)
    return pl.BlockSpec(shape, lambda b, s: (0,) * nd, pipeline_mode=pl.Buffered(1))


def _mixer_operands(g_mix, w_in, b_in, sgu_ln_g, sgu_ln_b, sgu_w_s, sgu_b_s, w_pa, pool_w,
                    pool_scale, w_pb, conv_w, conv_b, conv_ln_g, conv_ln_b, w_pc, w_out):
    L = w_in.shape[0]
    row = lambda a: a.reshape(1, -1)
    win_p, wpa_p, wpb_p, wpc_p, wout_p = (_pack_bf16_rows(w) for w in (w_in, w_pa, w_pb, w_pc, w_out))
    pool_p = _pack_bf16_rows(pool_w.reshape(L, POOL_GROUPS * POOL_GC, POOL_GC))
    pool_p = pool_p.reshape(L, POOL_GROUPS, POOL_GC // 2, POOL_GC)
    return [(row(g_mix[l]), win_p[l], row(b_in[l]), row(sgu_ln_g[l]), row(sgu_ln_b[l]), sgu_w_s[l],
             jnp.transpose(sgu_b_s[l]), wpa_p[l], pool_p[l], row(pool_scale[l]), wpb_p[l],
             conv_w[l], row(conv_b[l]), row(conv_ln_g[l]), row(conv_ln_b[l]), wpc_p[l], wout_p[l])
            for l in range(L)]


def _mixer_layer(x, ada, operands):
    B, S, D = x.shape
    tm = TM_MIX
    tile = pl.BlockSpec((None, tm, D), lambda b, s: (b, s, 0))
    return pl.pallas_call(
        _mixer_kernel,
        grid=(B, S // tm),
        in_specs=[tile, pl.BlockSpec((None, 6, D), lambda b, s: (b, 0, 0))]
                 + [_resident(a.shape) for a in operands],
        out_specs=tile,
        out_shape=jax.ShapeDtypeStruct(x.shape, x.dtype),
        scratch_shapes=[
            pltpu.VMEM((SGU_GROUPS, CHUNK, CHUNK), jnp.bfloat16),
            pltpu.VMEM((tm, D), jnp.bfloat16),
            pltpu.VMEM((tm, D), jnp.float32),
            pltpu.VMEM((HALO + tm, D), jnp.float32),
            pltpu.VMEM((2, tm + HALO - SUBLANES, POOL_GC), jnp.float32),
            pltpu.VMEM((HALO + tm, D), jnp.float32),
            pltpu.VMEM((SUBLANES - 1, tm + SHIFT_ROWS_EXTRA, D), jnp.float32),
            pltpu.VMEM((tm, D), jnp.float32),
            pltpu.VMEM((tm, D), jnp.float32),
        ],
        compiler_params=pltpu.CompilerParams(
            dimension_semantics=("arbitrary", "arbitrary"), vmem_limit_bytes=VMEM_LIMIT_BYTES),
        name="mixer_layer",
    )(x, ada, *operands)


def _ffn_kernel(x_ref, ada_ref, g_ref, win_ref, wout_ref, gfin_ref, o_ref, *, final_norm):
    x = x_ref[...]
    h = _modulated_rmsnorm(x, g_ref[...], ada_ref[ADA_SC_F:ADA_SC_F + 1, :],
                           ada_ref[ADA_SH_F:ADA_SH_F + 1, :]).astype(jnp.bfloat16)
    g_part = _wdot(h, win_ref[:, 0:D_FF])
    u_part = _wdot(h, win_ref[:, D_FF:2 * D_FF])
    act = (_silu(g_part) * u_part).astype(jnp.bfloat16)
    y = x + ada_ref[ADA_GT_F:ADA_GT_F + 1, :] * _wdot(act, wout_ref[...])
    if final_norm:
        y = y * lax.rsqrt(jnp.mean(y * y, axis=-1, keepdims=True) + EPS) * gfin_ref[...]
    o_ref[...] = y


def _ffn_operands(g_ffn, w_ffn_in, w_ffn_out, g_final):
    win_p, wout_p = _pack_bf16_rows(w_ffn_in), _pack_bf16_rows(w_ffn_out)
    return [(g_ffn[l].reshape(1, -1), win_p[l], wout_p[l], g_final.reshape(1, -1))
            for l in range(w_ffn_in.shape[0])]


def _ffn_layer(x, ada, operands, final_norm):
    B, S, D = x.shape
    tm = TM_FFN
    tile = pl.BlockSpec((None, tm, D), lambda b, s: (b, s, 0))
    return pl.pallas_call(
        functools.partial(_ffn_kernel, final_norm=final_norm),
        grid=(B, S // tm),
        in_specs=[tile, pl.BlockSpec((None, 6, D), lambda b, s: (b, 0, 0))]
                 + [_resident(a.shape) for a in operands],
        out_specs=tile,
        out_shape=jax.ShapeDtypeStruct(x.shape, x.dtype),
        compiler_params=pltpu.CompilerParams(
            dimension_semantics=("arbitrary", "arbitrary"), vmem_limit_bytes=VMEM_LIMIT_BYTES),
        name="ffn_layer",
    )(x, ada, *operands)


def kernel(x, c, w_ada, b_ada, g_mix, w_in, b_in, sgu_ln_g, sgu_ln_b, sgu_w_s, sgu_b_s, w_pa, pool_w, pool_scale, w_pb, conv_w, conv_b, conv_ln_g, conv_ln_b, w_pc, w_out, g_ffn, w_ffn_in, w_ffn_out, g_final):
    B, S, D = x.shape
    assert D == D_MODEL and S % TM_MIX == 0 and S % TM_FFN == 0 and TM_MIX % CHUNK == 0
    ada = _ada_table(c, w_ada, b_ada).reshape(DEPTH, B, 6, D)
    mixer_ops = _mixer_operands(g_mix, w_in, b_in, sgu_ln_g, sgu_ln_b, sgu_w_s, sgu_b_s, w_pa,
                                pool_w, pool_scale, w_pb, conv_w, conv_b, conv_ln_g, conv_ln_b,
                                w_pc, w_out)
    ffn_ops = _ffn_operands(g_ffn, w_ffn_in, w_ffn_out, g_final)
    for l in range(DEPTH):
        x = _mixer_layer(x, ada[l], mixer_ops[l])
        x = _ffn_layer(x, ada[l], ffn_ops[l], final_norm=(l == DEPTH - 1))
    return x
```

```python
import functools

import jax
import jax.numpy as jnp
from jax import lax
from jax.experimental import pallas as pl
from jax.experimental.pallas import tpu as pltpu

D_MODEL = 1024
DEPTH = 2
CHUNK = 128
SGU_GROUPS = 8
SGU_GC = D_MODEL // SGU_GROUPS
POOL_WINDOWS = (2, 4, 8, 16)
POOL_GROUPS = len(POOL_WINDOWS)
POOL_GC = D_MODEL // POOL_GROUPS
CONV_WIDTH = 31
D_FF = 2816
EPS = 1e-6

OFF_U = 0
OFF_V = D_MODEL
OFF_POOL = 2 * D_MODEL
OFF_GLU_A = 3 * D_MODEL
OFF_GLU_G = 4 * D_MODEL
OFF_GATE = 5 * D_MODEL
D_IN = 8 * D_MODEL
N_BRANCH = 3

SUBLANES = 8
LANES = 128
BF16_ROWS = 16
PACK_BLOCK_ROWS = 512
PACK_BLOCK_COLS = 1024
HALO = 32
SHIFT_ROWS_EXTRA = HALO - SUBLANES
TM_MIX = 256
TM_FFN = 512
BLK = 256
N_BLK = D_MODEL // BLK
CONV_ROWS = 32
SCHED_LAG = 3
SLOT_VECTOR_BUDGET_PCT = 110
VMEM_LIMIT_BYTES = 58 * 1024 * 1024

ADA_SH_M, ADA_SC_M, ADA_GT_M, ADA_SH_F, ADA_SC_F, ADA_GT_F = range(6)


def _pack_kernel(w_ref, o_ref):
    o_ref[...] = pltpu.bitcast(w_ref[...].astype(jnp.bfloat16), jnp.uint32)


def _largest_divisor(n, multiple_of, at_most):
    return max(d for d in range(multiple_of, at_most + 1, multiple_of) if n % d == 0)


def _pack_bf16_rows(w):
    L, K, N = w.shape
    kb = _largest_divisor(K, 2 * BF16_ROWS, PACK_BLOCK_ROWS)
    nb = _largest_divisor(N, LANES, PACK_BLOCK_COLS)
    return pl.pallas_call(
        _pack_kernel,
        grid=(L, K // kb, N // nb),
        in_specs=[pl.BlockSpec((None, kb, nb), lambda l, i, j: (l, i, j))],
        out_specs=pl.BlockSpec((None, kb // 2, nb), lambda l, i, j: (l, i, j)),
        out_shape=jax.ShapeDtypeStruct((L, K // 2, N), jnp.uint32),
        compiler_params=pltpu.CompilerParams(
            dimension_semantics=("arbitrary", "arbitrary", "arbitrary")),
        name="pack_bf16_rows",
    )(w)


def _wdot(a, w_packed):
    return jnp.dot(a, pltpu.bitcast(w_packed, jnp.bfloat16), preferred_element_type=jnp.float32)


def _sigmoid(x):
    return 0.5 * jnp.tanh(0.5 * x) + 0.5


def _silu(x):
    hx = 0.5 * x
    return hx * jnp.tanh(hx) + hx


def _gelu(x):
    hx = 0.5 * x
    return hx * lax.erf(x * (2.0 ** -0.5)) + hx


def _layernorm(x, g, b):
    mu = jnp.mean(x, axis=-1, keepdims=True)
    xc = x - mu
    var = jnp.mean(xc * xc, axis=-1, keepdims=True)
    return xc * lax.rsqrt(var + EPS) * g + b


def _modulated_rmsnorm(x, g, scale, shift):
    y = x * lax.rsqrt(jnp.mean(x * x, axis=-1, keepdims=True) + EPS) * g
    return y * (1.0 + scale) + shift


def _token_zero(token):
    bits = pltpu.bitcast(token, jnp.uint32)
    bits = lax.shift_right_logical(lax.shift_right_logical(bits, jnp.uint32(16)), jnp.uint32(16))
    return pltpu.bitcast(bits, jnp.float32)


def _after_f32(val, token):
    if token is None:
        return val
    head = val[0:SUBLANES, 0:LANES] + _token_zero(token)
    if val.shape[1] > LANES:
        head = jnp.concatenate([head, val[0:SUBLANES, LANES:]], axis=1)
    if val.shape[0] > SUBLANES:
        head = jnp.concatenate([head, val[SUBLANES:]], axis=0)
    return head


def _after_bf16(val, token):
    if token is None:
        return val
    zero = _token_zero(token)
    zero = jnp.concatenate([zero, zero], axis=0).astype(jnp.bfloat16)
    head = val[0:BF16_ROWS, 0:LANES] + zero
    if val.shape[1] > LANES:
        head = jnp.concatenate([head, val[0:BF16_ROWS, LANES:]], axis=1)
    if val.shape[0] > BF16_ROWS:
        head = jnp.concatenate([head, val[BF16_ROWS:]], axis=0)
    return head


def _token_of(val):
    return val[val.shape[0] - SUBLANES:, val.shape[1] - LANES:]


def _ada_kernel(c_ref, w_ref, b_ref, o_ref):
    c = c_ref[...]
    c_act = c / (1.0 + jnp.exp(-c))
    o_ref[...] = jnp.dot(c_act, w_ref[...], preferred_element_type=jnp.float32,
                         precision=lax.Precision.HIGHEST) + b_ref[...]


def _ada_table(c, w_ada, b_ada):
    L, D, D6 = w_ada.shape
    B = c.shape[0]
    nblk = D6 // D
    return pl.pallas_call(
        _ada_kernel,
        grid=(L, nblk),
        in_specs=[
            pl.BlockSpec((B, D), lambda l, j: (0, 0)),
            pl.BlockSpec((None, D, D), lambda l, j: (l, 0, j)),
            pl.BlockSpec((None, 1, D), lambda l, j: (l, 0, j)),
        ],
        out_specs=pl.BlockSpec((None, B, D), lambda l, j: (l, 0, j)),
        out_shape=jax.ShapeDtypeStruct((L, B, D6), jnp.float32),
        compiler_params=pltpu.CompilerParams(dimension_semantics=("arbitrary", "arbitrary")),
        name="ada_table",
    )(c, w_ada, b_ada.reshape(L, 1, D6))


def _mixer_kernel(x_ref, ada_ref, gmix_ref, win_ref, bin_ref, lng_ref, lnb_ref, ws_ref, bs_ref,
                  wpa_ref, poolw_ref, pscale_ref, wpb_ref, convw_ref, convb_ref, clng_ref,
                  clnb_ref, wpc_ref, wout_ref, o_ref,
                  wm_scr, h_scr, vb_scr, sa_scr, pooled_scr, ypool_scr, yconv_scr, merged_scr,
                  mix_scr, pbuf, pool_lvl, zbuf, zsh, conv_scr, gate_scr, acc_scr):
    tm = x_ref.shape[0]
    b_id = pl.program_id(0)
    s_id = pl.program_id(1)

    @pl.when(jnp.logical_and(b_id == 0, s_id == 0))
    def _():
        row = lax.broadcasted_iota(jnp.int32, (CHUNK, CHUNK), 0)
        col = lax.broadcasted_iota(jnp.int32, (CHUNK, CHUNK), 1)
        for g in range(SGU_GROUPS):
            wm_scr[g] = jnp.where(row >= col, ws_ref[g], 0.0).astype(jnp.bfloat16)

    @pl.when(s_id == 0)
    def _():
        pbuf[0:HALO, :] = jnp.zeros((HALO, D_MODEL), jnp.float32)
        zbuf[0:HALO, :] = jnp.zeros((HALO, D_MODEL), jnp.float32)

    h_scr[...] = _modulated_rmsnorm(
        x_ref[...], gmix_ref[...], ada_ref[ADA_SC_M:ADA_SC_M + 1, :],
        ada_ref[ADA_SH_M:ADA_SH_M + 1, :]).astype(jnp.bfloat16)

    blk = lambda j: slice(j * BLK, (j + 1) * BLK)
    body = slice(HALO, HALO + tm)


    def proj_block(off, j, token):
        lo = off + j * BLK
        r = _wdot(_after_bf16(h_scr[...], token), win_ref[:, lo:lo + BLK])
        return r + bin_ref[:, lo:lo + BLK]

    def glu_value(j):
        def unit(token):
            a = proj_block(OFF_GLU_A, j, token)
            conv_scr[:, blk(j)] = a
            return _token_of(a)
        return unit

    def glu_gate(j):
        def unit(token):
            g = proj_block(OFF_GLU_G, j, token)
            z = conv_scr[:, blk(j)] * _sigmoid(g)
            zbuf[body, blk(j)] = z
            return _token_of(z)
        return unit

    def sgu_v(j):
        def unit(token):
            v = _gelu(proj_block(OFF_V, j, token))
            mix_scr[:, blk(j)] = v
            return _token_of(v)
        return unit

    def sgu_mix(token):
        n_chunks = tm // CHUNK
        tok = None
        for g in range(SGU_GROUPS):
            cols = slice(g * SGU_GC, (g + 1) * SGU_GC)
            rhs = jnp.concatenate(
                [vb_scr[n * CHUNK:(n + 1) * CHUNK, cols] for n in range(n_chunks)], axis=1)
            lhs = _after_bf16(wm_scr[g], token if g == 0 else None)
            mixed = jnp.dot(lhs, rhs, preferred_element_type=jnp.float32) + bs_ref[:, g:g + 1]
            for n in range(n_chunks):
                mix_scr[n * CHUNK:(n + 1) * CHUNK, cols] = mixed[:, n * SGU_GC:(n + 1) * SGU_GC]
            tok = _token_of(mixed)
        return tok

    def sgu_u(j):
        def unit(token):
            s = _gelu(proj_block(OFF_U, j, token)) * mix_scr[:, blk(j)]
            sa_scr[:, blk(j)] = s.astype(jnp.bfloat16)
            return _token_of(s)
        return unit

    def pool_in(j):
        def unit(token):
            p = proj_block(OFF_POOL, j, token)
            pbuf[body, blk(j)] = p
            return _token_of(p)
        return unit

    def branch_gate(i, j):
        def unit(token):
            g = _sigmoid(proj_block(OFF_GATE + i * D_MODEL, j, token))
            gate_scr[i, :, blk(j)] = g
            return _token_of(g)
        return unit

    def pool_mix(token):
        for i in range(POOL_GROUPS):
            cols = slice(i * POOL_GC, (i + 1) * POOL_GC)
            lhs = _after_bf16(pooled_scr[:, cols], token if i == 0 else None)
            y = _wdot(lhs, poolw_ref[i]) * pscale_ref[:, cols]
            ypool_scr[:, cols] = y.astype(jnp.bfloat16)
        return _token_of(y)

    def branch_out(i, src_scr, w_ref, j):
        def unit(token):
            y = gate_scr[i, :, blk(j)] * _wdot(_after_bf16(src_scr[...], token), w_ref[:, blk(j)])
            if i == 0:
                acc_scr[:, blk(j)] = y
            elif i < N_BRANCH - 1:
                acc_scr[:, blk(j)] = acc_scr[:, blk(j)] + y
            else:
                y = acc_scr[:, blk(j)] + y
                merged_scr[:, blk(j)] = y.astype(jnp.bfloat16)
            return _token_of(y)
        return unit

    def out_block(j):
        def unit(token):
            y = _wdot(_after_bf16(merged_scr[...], token), wout_ref[:, blk(j)])
            y = x_ref[:, blk(j)] + ada_ref[ADA_GT_M:ADA_GT_M + 1, blk(j)] * y
            o_ref[:, blk(j)] = y
            return _token_of(y)
        return unit


    def shifted_copy(j, shift):
        def unit(token):
            n = tm + SHIFT_ROWS_EXTRA
            if token is not None:
                first = (slice(HALO, HALO + SUBLANES), blk(j))
                zbuf[first] = _after_f32(zbuf[first], token)
            zsh[shift - 1, :, blk(j)] = zbuf[shift:shift + n, blk(j)]
            return zsh[shift - 1, n - SUBLANES:n, (j + 1) * BLK - LANES:(j + 1) * BLK]
        return unit

    def conv_unit(j, r0):
        def unit(token):
            acc = jnp.zeros((CONV_ROWS, BLK), jnp.float32) + convb_ref[:, blk(j)]
            for k in range(CONV_WIDTH):
                off = HALO - (CONV_WIDTH - 1) + k
                shift = off % SUBLANES
                base = off - shift + r0
                if shift == 0:
                    src = zbuf[base:base + CONV_ROWS, blk(j)]
                else:
                    src = zsh[shift - 1, base:base + CONV_ROWS, blk(j)]
                if k == 0:
                    src = _after_f32(src, token)
                w8 = convw_ref[SUBLANES * k:SUBLANES * (k + 1), blk(j)]
                prod = src.reshape(CONV_ROWS // SUBLANES, SUBLANES, BLK) * w8[None]
                acc = acc + prod.reshape(CONV_ROWS, BLK)
            conv_scr[r0:r0 + CONV_ROWS, blk(j)] = acc
            return _token_of(acc)
        return unit

    def conv_history(token):
        z = _after_f32(zbuf[tm:tm + HALO, :], token)
        zbuf[0:HALO, :] = z
        return _token_of(z)

    def conv_norm(r0, rows):
        def unit(token):
            c = _after_f32(conv_scr[r0:r0 + rows, :], token)
            y = _silu(_layernorm(c, clng_ref[...], clnb_ref[...]))
            yconv_scr[r0:r0 + rows, :] = y.astype(jnp.bfloat16)
            return _token_of(y)
        return unit

    def sgu_norm(r0, rows):
        def unit(token):
            v = _after_f32(mix_scr[r0:r0 + rows, :], token)
            y = _layernorm(v, lng_ref[...], lnb_ref[...])
            vb_scr[r0:r0 + rows, :] = y.astype(jnp.bfloat16)
            return _token_of(y)
        return unit

    def window_sum(cols, win, token):
        first = {win: 0}
        k = win // 2
        while k >= 1:
            first[k] = -(-(k - first[2 * k]) // SUBLANES) * -SUBLANES
            k //= 2
        src, src_first, k = pbuf, -HALO, 1
        src_cols = cols
        while True:
            lo = first[2 * k] - src_first
            n = tm - first[2 * k]
            s = _after_f32(src[lo:lo + n, src_cols], token) + src[lo - k:lo - k + n, src_cols]
            token = None
            if 2 * k == win:
                return s
            slot = pool_lvl.at[(2 * k).bit_length() % 2]
            slot[0:n, :] = s
            src, src_first, src_cols, k = slot, first[2 * k], slice(None), 2 * k

    def pool_window(i):
        def unit(token):
            win = POOL_WINDOWS[i]
            cols = slice(i * POOL_GC, (i + 1) * POOL_GC)
            pos = s_id * tm + lax.broadcasted_iota(jnp.int32, (tm, 1), 0) + 1
            inv_cnt = 1.0 / jnp.minimum(pos, win).astype(jnp.float32)
            pooled = window_sum(cols, win, token) * inv_cnt - pbuf[body, cols]
            pooled_scr[:, cols] = pooled.astype(jnp.bfloat16)
            return _token_of(pooled)
        return unit

    def pool_history(token):
        p = _after_f32(pbuf[tm:tm + HALO, :], token)
        pbuf[0:HALO, :] = p
        return _token_of(p)


    matmul_units = []
    for j in range(N_BLK):
        matmul_units += [(glu_value(j), 0.1), (glu_gate(j), 0.45)]
    slot_glu_done = {j: 2 * j + 1 for j in range(N_BLK)}
    matmul_units += [(sgu_v(j), 0.4) for j in range(N_BLK)]
    slot_v_done = len(matmul_units) - 1
    matmul_units += [(pool_in(j), 0.1) for j in range(N_BLK)]
    slot_p_done = len(matmul_units) - 1
    for i in range(N_BRANCH):
        matmul_units += [(branch_gate(i, j), 0.35) for j in range(N_BLK)]
    slot_sgu_mix = len(matmul_units)
    matmul_units += [(sgu_mix, 0.3)]
    matmul_units += [(sgu_u(j), 0.5) for j in range(N_BLK)]
    matmul_units += [(branch_out(0, sa_scr, wpa_ref, j), 0.25) for j in range(N_BLK)]
    slot_pool_mix = len(matmul_units)
    matmul_units += [(pool_mix, 0.3)]
    matmul_units += [(branch_out(1, ypool_scr, wpb_ref, j), 0.3) for j in range(N_BLK)]
    slot_conv_out = len(matmul_units)
    matmul_units += [(branch_out(2, yconv_scr, wpc_ref, j), 0.3) for j in range(N_BLK)]
    matmul_units += [(out_block(j), 0.2) for j in range(N_BLK)]
    n_slots = len(matmul_units)

    vector_units = [[] for _ in range(n_slots)]
    load = [cost for _, cost in matmul_units]

    def place(unit, cost, earliest, latest):
        slots = range(earliest, latest + 1)
        slot = next((s for s in slots if load[s] + cost <= SLOT_VECTOR_BUDGET_PCT / 100),
                    min(slots, key=lambda s: load[s]))
        vector_units[slot].append(unit)
        load[slot] += cost
        return slot

    half = tm // 2
    s = place(sgu_norm(0, half), 0.7, slot_v_done + 1, slot_sgu_mix - 2)
    place(sgu_norm(half, half), 0.7, s, slot_sgu_mix - 1)
    s = slot_p_done + 1
    for i, cost in enumerate((0.25, 0.5, 0.75, 1.0)):
        s = place(pool_window(i), cost, s, slot_pool_mix - 2)
    place(pool_history, 0.1, s, slot_pool_mix - 1)

    s = 0
    for j in range(N_BLK):
        s = max(s, slot_glu_done[j] + 1)
        for shift in range(1, SUBLANES):
            s = place(shifted_copy(j, shift), 0.18, s, slot_conv_out - 4)
        for r0 in range(0, tm, CONV_ROWS):
            s = place(conv_unit(j, r0), 0.5, s, slot_conv_out - 3)
    s = place(conv_history, 0.1, s, slot_conv_out - 3)
    s = place(conv_norm(0, half), 1.5, s, slot_conv_out - 2)
    place(conv_norm(half, half), 1.5, s, slot_conv_out - 1)

    matmul_tokens, vector_tokens = {}, {}
    for s in range(n_slots):
        matmul_tokens[s] = matmul_units[s][0](vector_tokens.get(s - SCHED_LAG))
        tok = vector_tokens.get(s - 1)
        for unit in vector_units[s]:
            tok = unit(matmul_tokens.get(s - SCHED_LAG))
        vector_tokens[s] = tok


def _resident(shape):
    nd = len(shape)
    return pl.BlockSpec(shape, lambda b, s: (0,) * nd, pipeline_mode=pl.Buffered(1))


def _mixer_operands(g_mix, w_in, b_in, sgu_ln_g, sgu_ln_b, sgu_w_s, sgu_b_s, w_pa, pool_w,
                    pool_scale, w_pb, conv_w, conv_b, conv_ln_g, conv_ln_b, w_pc, w_out):
    L = w_in.shape[0]
    row = lambda a: a.reshape(1, -1)
    win_p, wpa_p, wpb_p, wpc_p, wout_p = (_pack_bf16_rows(w) for w in (w_in, w_pa, w_pb, w_pc, w_out))
    pool_p = _pack_bf16_rows(pool_w.reshape(L, POOL_GROUPS * POOL_GC, POOL_GC))
    pool_p = pool_p.reshape(L, POOL_GROUPS, POOL_GC // 2, POOL_GC)
    conv_w8 = jnp.repeat(conv_w, SUBLANES, axis=1)
    return [(row(g_mix[l]), win_p[l], row(b_in[l]), row(sgu_ln_g[l]), row(sgu_ln_b[l]), sgu_w_s[l],
             jnp.transpose(sgu_b_s[l]), wpa_p[l], pool_p[l], row(pool_scale[l]), wpb_p[l],
             conv_w8[l], row(conv_b[l]), row(conv_ln_g[l]), row(conv_ln_b[l]), wpc_p[l], wout_p[l])
            for l in range(L)]


def _mixer_layer(x, ada, operands):
    B, S, D = x.shape
    tm = TM_MIX
    tile = pl.BlockSpec((None, tm, D), lambda b, s: (b, s, 0))
    bf16_tile = pltpu.VMEM((tm, D), jnp.bfloat16)
    return pl.pallas_call(
        _mixer_kernel,
        grid=(B, S // tm),
        in_specs=[tile, pl.BlockSpec((None, 6, D), lambda b, s: (b, 0, 0))]
                 + [_resident(a.shape) for a in operands],
        out_specs=tile,
        out_shape=jax.ShapeDtypeStruct(x.shape, x.dtype),
        scratch_shapes=[
            pltpu.VMEM((SGU_GROUPS, CHUNK, CHUNK), jnp.bfloat16),
            bf16_tile,
            bf16_tile,
            bf16_tile,
            bf16_tile,
            bf16_tile,
            bf16_tile,
            bf16_tile,
            pltpu.VMEM((tm, D), jnp.float32),
            pltpu.VMEM((HALO + tm, D), jnp.float32),
            pltpu.VMEM((2, tm + HALO - SUBLANES, POOL_GC), jnp.float32),
            pltpu.VMEM((HALO + tm, D), jnp.float32),
            pltpu.VMEM((SUBLANES - 1, tm + SHIFT_ROWS_EXTRA, D), jnp.float32),
            pltpu.VMEM((tm, D), jnp.float32),
            pltpu.VMEM((N_BRANCH, tm, D), jnp.float32),
            pltpu.VMEM((tm, D), jnp.float32),
        ],
        compiler_params=pltpu.CompilerParams(
            dimension_semantics=("arbitrary", "arbitrary"), vmem_limit_bytes=VMEM_LIMIT_BYTES),
        name="mixer_layer",
    )(x, ada, *operands)


def _ffn_kernel(x_ref, ada_ref, g_ref, win_ref, wout_ref, gfin_ref, o_ref, *, final_norm):
    x = x_ref[...]
    h = _modulated_rmsnorm(x, g_ref[...], ada_ref[ADA_SC_F:ADA_SC_F + 1, :],
                           ada_ref[ADA_SH_F:ADA_SH_F + 1, :]).astype(jnp.bfloat16)
    g_part = _wdot(h, win_ref[:, 0:D_FF])
    u_part = _wdot(h, win_ref[:, D_FF:2 * D_FF])
    act = (_silu(g_part) * u_part).astype(jnp.bfloat16)
    y = x + ada_ref[ADA_GT_F:ADA_GT_F + 1, :] * _wdot(act, wout_ref[...])
    if final_norm:
        y = y * lax.rsqrt(jnp.mean(y * y, axis=-1, keepdims=True) + EPS) * gfin_ref[...]
    o_ref[...] = y


def _ffn_operands(g_ffn, w_ffn_in, w_ffn_out, g_final):
    win_p, wout_p = _pack_bf16_rows(w_ffn_in), _pack_bf16_rows(w_ffn_out)
    return [(g_ffn[l].reshape(1, -1), win_p[l], wout_p[l], g_final.reshape(1, -1))
            for l in range(w_ffn_in.shape[0])]


def _ffn_layer(x, ada, operands, final_norm):
    B, S, D = x.shape
    tm = TM_FFN
    tile = pl.BlockSpec((None, tm, D), lambda b, s: (b, s, 0))
    return pl.pallas_call(
        functools.partial(_ffn_kernel, final_norm=final_norm),
        grid=(B, S // tm),
        in_specs=[tile, pl.BlockSpec((None, 6, D), lambda b, s: (b, 0, 0))]
                 + [_resident(a.shape) for a in operands],
        out_specs=tile,
        out_shape=jax.ShapeDtypeStruct(x.shape, x.dtype),
        compiler_params=pltpu.CompilerParams(
            dimension_semantics=("arbitrary", "arbitrary"), vmem_limit_bytes=VMEM_LIMIT_BYTES),
        name="ffn_layer",
    )(x, ada, *operands)


def kernel(x, c, w_ada, b_ada, g_mix, w_in, b_in, sgu_ln_g, sgu_ln_b, sgu_w_s, sgu_b_s, w_pa, pool_w, pool_scale, w_pb, conv_w, conv_b, conv_ln_g, conv_ln_b, w_pc, w_out, g_ffn, w_ffn_in, w_ffn_out, g_final):
    B, S, D = x.shape
    assert D == D_MODEL and S % TM_MIX == 0 and S % TM_FFN == 0 and TM_MIX % CHUNK == 0
    ada = _ada_table(c, w_ada, b_ada).reshape(DEPTH, B, 6, D)
    mixer_ops = _mixer_operands(g_mix, w_in, b_in, sgu_ln_g, sgu_ln_b, sgu_w_s, sgu_b_s, w_pa,
                                pool_w, pool_scale, w_pb, conv_w, conv_b, conv_ln_g, conv_ln_b,
                                w_pc, w_out)
    ffn_ops = _ffn_operands(g_ffn, w_ffn_in, w_ffn_out, g_final)
    for l in range(DEPTH):
        x = _mixer_layer(x, ada[l], mixer_ops[l])
        x = _ffn_layer(x, ada[l], ffn_ops[l], final_norm=(l == DEPTH - 1))
    return x
```

```python
import functools

import jax
import jax.numpy as jnp
from jax import lax
from jax.experimental import pallas as pl
from jax.experimental.pallas import tpu as pltpu

D_MODEL = 1024
DEPTH = 2
CHUNK = 128
SGU_GROUPS = 8
SGU_GC = D_MODEL // SGU_GROUPS
POOL_WINDOWS = (2, 4, 8, 16)
POOL_GROUPS = len(POOL_WINDOWS)
POOL_GC = D_MODEL // POOL_GROUPS
CONV_WIDTH = 31
D_FF = 2816
EPS = 1e-6

OFF_U = 0
OFF_V = D_MODEL
OFF_POOL = 2 * D_MODEL
OFF_GLU_A = 3 * D_MODEL
OFF_GLU_G = 4 * D_MODEL
OFF_GATE = 5 * D_MODEL
D_IN = 8 * D_MODEL
N_BRANCH = 3

SUBLANES = 8
LANES = 128
BF16_ROWS = 16
PACK_BLOCK_ROWS = 1024
PACK_BLOCK_COLS = 2048
PACK_VMEM_LIMIT_BYTES = 40 * 1024 * 1024
HALO = 32
SHIFT_ROWS_EXTRA = HALO - SUBLANES
TM_MIX = 256
TM_FFN = 512
BLK = 256
N_BLK = D_MODEL // BLK
CONV_ROWS = 32
SCHED_LAG = 3
SLOT_VECTOR_BUDGET_PCT = 110
VMEM_LIMIT_BYTES = 58 * 1024 * 1024

ADA_SH_M, ADA_SC_M, ADA_GT_M, ADA_SH_F, ADA_SC_F, ADA_GT_F = range(6)


def _pack_kernel(w_ref, o_ref):
    o_ref[...] = pltpu.bitcast(w_ref[...].astype(jnp.bfloat16), jnp.uint32)


def _largest_divisor(n, multiple_of, at_most):
    return max(d for d in range(multiple_of, at_most + 1, multiple_of) if n % d == 0)


def _pack_bf16_rows(w):
    L, K, N = w.shape
    kb = _largest_divisor(K, 2 * BF16_ROWS, PACK_BLOCK_ROWS)
    nb = _largest_divisor(N, LANES, PACK_BLOCK_COLS)
    return pl.pallas_call(
        _pack_kernel,
        grid=(L, K // kb, N // nb),
        in_specs=[pl.BlockSpec((None, kb, nb), lambda l, i, j: (l, i, j))],
        out_specs=pl.BlockSpec((None, kb // 2, nb), lambda l, i, j: (l, i, j)),
        out_shape=jax.ShapeDtypeStruct((L, K // 2, N), jnp.uint32),
        compiler_params=pltpu.CompilerParams(
            dimension_semantics=("arbitrary", "arbitrary", "arbitrary"),
            vmem_limit_bytes=PACK_VMEM_LIMIT_BYTES),
        name="pack_bf16_rows",
    )(w)


def _wdot(a, w_packed):
    return jnp.dot(a, pltpu.bitcast(w_packed, jnp.bfloat16), preferred_element_type=jnp.float32)


def _sigmoid(x):
    return 0.5 * jnp.tanh(0.5 * x) + 0.5


def _silu(x):
    hx = 0.5 * x
    return hx * jnp.tanh(hx) + hx


def _gelu(x):
    hx = 0.5 * x
    return hx * lax.erf(x * (2.0 ** -0.5)) + hx


def _layernorm(x, g, b):
    mu = jnp.mean(x, axis=-1, keepdims=True)
    xc = x - mu
    var = jnp.mean(xc * xc, axis=-1, keepdims=True)
    return xc * lax.rsqrt(var + EPS) * g + b


def _modulated_rmsnorm(x, g, scale, shift):
    y = x * lax.rsqrt(jnp.mean(x * x, axis=-1, keepdims=True) + EPS) * g
    return y * (1.0 + scale) + shift


def _token_zero(token):
    bits = pltpu.bitcast(token, jnp.uint32)
    bits = lax.shift_right_logical(lax.shift_right_logical(bits, jnp.uint32(16)), jnp.uint32(16))
    return pltpu.bitcast(bits, jnp.float32)


def _after_f32(val, token):
    if token is None:
        return val
    head = val[0:SUBLANES, 0:LANES] + _token_zero(token)
    if val.shape[1] > LANES:
        head = jnp.concatenate([head, val[0:SUBLANES, LANES:]], axis=1)
    if val.shape[0] > SUBLANES:
        head = jnp.concatenate([head, val[SUBLANES:]], axis=0)
    return head


def _after_bf16(val, token):
    if token is None:
        return val
    zero = _token_zero(token)
    zero = jnp.concatenate([zero, zero], axis=0).astype(jnp.bfloat16)
    head = val[0:BF16_ROWS, 0:LANES] + zero
    if val.shape[1] > LANES:
        head = jnp.concatenate([head, val[0:BF16_ROWS, LANES:]], axis=1)
    if val.shape[0] > BF16_ROWS:
        head = jnp.concatenate([head, val[BF16_ROWS:]], axis=0)
    return head


def _token_of(val):
    return val[val.shape[0] - SUBLANES:, val.shape[1] - LANES:]


def _ada_kernel(c_ref, w_ref, b_ref, o_ref):
    c = c_ref[...]
    c_act = c / (1.0 + jnp.exp(-c))
    o_ref[...] = jnp.dot(c_act, w_ref[...], preferred_element_type=jnp.float32,
                         precision=lax.Precision.HIGHEST) + b_ref[...]


def _ada_table(c, w_ada, b_ada):
    L, D, D6 = w_ada.shape
    B = c.shape[0]
    nblk = D6 // D
    return pl.pallas_call(
        _ada_kernel,
        grid=(L, nblk),
        in_specs=[
            pl.BlockSpec((B, D), lambda l, j: (0, 0)),
            pl.BlockSpec((None, D, D), lambda l, j: (l, 0, j)),
            pl.BlockSpec((None, 1, D), lambda l, j: (l, 0, j)),
        ],
        out_specs=pl.BlockSpec((None, B, D), lambda l, j: (l, 0, j)),
        out_shape=jax.ShapeDtypeStruct((L, B, D6), jnp.float32),
        compiler_params=pltpu.CompilerParams(dimension_semantics=("arbitrary", "arbitrary")),
        name="ada_table",
    )(c, w_ada, b_ada.reshape(L, 1, D6))


def _mixer_kernel(x_ref, ada_ref, gmix_ref, win_ref, bin_ref, lng_ref, lnb_ref, ws_ref, bs_ref,
                  wpa_ref, poolw_ref, pscale_ref, wpb_ref, convw_ref, convb_ref, clng_ref,
                  clnb_ref, wpc_ref, wout_ref, o_ref,
                  wm_scr, h_scr, vb_scr, sa_scr, pooled_scr, ypool_scr, yconv_scr, merged_scr,
                  mix_scr, pbuf, pool_lvl, zbuf, zsh, conv_scr, gate_scr, acc_scr):
    tm = x_ref.shape[0]
    b_id = pl.program_id(0)
    s_id = pl.program_id(1)

    @pl.when(jnp.logical_and(b_id == 0, s_id == 0))
    def _():
        row = lax.broadcasted_iota(jnp.int32, (CHUNK, CHUNK), 0)
        col = lax.broadcasted_iota(jnp.int32, (CHUNK, CHUNK), 1)
        for g in range(SGU_GROUPS):
            wm_scr[g] = jnp.where(row >= col, ws_ref[g], 0.0).astype(jnp.bfloat16)

    @pl.when(s_id == 0)
    def _():
        pbuf[0:HALO, :] = jnp.zeros((HALO, D_MODEL), jnp.float32)
        zbuf[0:HALO, :] = jnp.zeros((HALO, D_MODEL), jnp.float32)

    h_scr[...] = _modulated_rmsnorm(
        x_ref[...], gmix_ref[...], ada_ref[ADA_SC_M:ADA_SC_M + 1, :],
        ada_ref[ADA_SH_M:ADA_SH_M + 1, :]).astype(jnp.bfloat16)

    blk = lambda j: slice(j * BLK, (j + 1) * BLK)
    body = slice(HALO, HALO + tm)


    def proj_block(off, j, token):
        lo = off + j * BLK
        r = _wdot(_after_bf16(h_scr[...], token), win_ref[:, lo:lo + BLK])
        return r + bin_ref[:, lo:lo + BLK]

    def glu_value(j):
        def unit(token):
            a = proj_block(OFF_GLU_A, j, token)
            conv_scr[:, blk(j)] = a
            return _token_of(a)
        return unit

    def glu_gate(j):
        def unit(token):
            g = proj_block(OFF_GLU_G, j, token)
            z = conv_scr[:, blk(j)] * _sigmoid(g)
            zbuf[body, blk(j)] = z
            return _token_of(z)
        return unit

    def sgu_v(j):
        def unit(token):
            v = _gelu(proj_block(OFF_V, j, token))
            mix_scr[:, blk(j)] = v
            return _token_of(v)
        return unit

    def sgu_mix(token):
        n_chunks = tm // CHUNK
        tok = None
        for g in range(SGU_GROUPS):
            cols = slice(g * SGU_GC, (g + 1) * SGU_GC)
            rhs = jnp.concatenate(
                [vb_scr[n * CHUNK:(n + 1) * CHUNK, cols] for n in range(n_chunks)], axis=1)
            lhs = _after_bf16(wm_scr[g], token if g == 0 else None)
            mixed = jnp.dot(lhs, rhs, preferred_element_type=jnp.float32) + bs_ref[:, g:g + 1]
            for n in range(n_chunks):
                mix_scr[n * CHUNK:(n + 1) * CHUNK, cols] = mixed[:, n * SGU_GC:(n + 1) * SGU_GC]
            tok = _token_of(mixed)
        return tok

    def sgu_u(j):
        def unit(token):
            s = _gelu(proj_block(OFF_U, j, token)) * mix_scr[:, blk(j)]
            sa_scr[:, blk(j)] = s.astype(jnp.bfloat16)
            return _token_of(s)
        return unit

    def pool_in(j):
        def unit(token):
            p = proj_block(OFF_POOL, j, token)
            pbuf[body, blk(j)] = p
            return _token_of(p)
        return unit

    def branch_gate(i, j):
        def unit(token):
            g = _sigmoid(proj_block(OFF_GATE + i * D_MODEL, j, token))
            gate_scr[i, :, blk(j)] = g
            return _token_of(g)
        return unit

    def pool_mix(token):
        for i in range(POOL_GROUPS):
            cols = slice(i * POOL_GC, (i + 1) * POOL_GC)
            lhs = _after_bf16(pooled_scr[:, cols], token if i == 0 else None)
            y = _wdot(lhs, poolw_ref[i]) * pscale_ref[:, cols]
            ypool_scr[:, cols] = y.astype(jnp.bfloat16)
        return _token_of(y)

    def branch_out(i, src_scr, w_ref, j):
        def unit(token):
            y = gate_scr[i, :, blk(j)] * _wdot(_after_bf16(src_scr[...], token), w_ref[:, blk(j)])
            if i == 0:
                acc_scr[:, blk(j)] = y
            elif i < N_BRANCH - 1:
                acc_scr[:, blk(j)] = acc_scr[:, blk(j)] + y
            else:
                y = acc_scr[:, blk(j)] + y
                merged_scr[:, blk(j)] = y.astype(jnp.bfloat16)
            return _token_of(y)
        return unit

    def out_block(j):
        def unit(token):
            y = _wdot(_after_bf16(merged_scr[...], token), wout_ref[:, blk(j)])
            y = x_ref[:, blk(j)] + ada_ref[ADA_GT_M:ADA_GT_M + 1, blk(j)] * y
            o_ref[:, blk(j)] = y
            return _token_of(y)
        return unit


    def shifted_copy(j, shift):
        def unit(token):
            n = tm + SHIFT_ROWS_EXTRA
            if token is not None:
                first = (slice(HALO, HALO + SUBLANES), blk(j))
                zbuf[first] = _after_f32(zbuf[first], token)
            zsh[shift - 1, :, blk(j)] = zbuf[shift:shift + n, blk(j)]
            return zsh[shift - 1, n - SUBLANES:n, (j + 1) * BLK - LANES:(j + 1) * BLK]
        return unit

    def conv_unit(j, r0):
        def unit(token):
            acc = jnp.zeros((CONV_ROWS, BLK), jnp.float32) + convb_ref[:, blk(j)]
            for k in range(CONV_WIDTH):
                off = HALO - (CONV_WIDTH - 1) + k
                shift = off % SUBLANES
                base = off - shift + r0
                if shift == 0:
                    src = zbuf[base:base + CONV_ROWS, blk(j)]
                else:
                    src = zsh[shift - 1, base:base + CONV_ROWS, blk(j)]
                if k == 0:
                    src = _after_f32(src, token)
                w8 = convw_ref[SUBLANES * k:SUBLANES * (k + 1), blk(j)]
                prod = src.reshape(CONV_ROWS // SUBLANES, SUBLANES, BLK) * w8[None]
                acc = acc + prod.reshape(CONV_ROWS, BLK)
            conv_scr[r0:r0 + CONV_ROWS, blk(j)] = acc
            return _token_of(acc)
        return unit

    def conv_history(token):
        z = _after_f32(zbuf[tm:tm + HALO, :], token)
        zbuf[0:HALO, :] = z
        return _token_of(z)

    def conv_norm(r0, rows):
        def unit(token):
            c = _after_f32(conv_scr[r0:r0 + rows, :], token)
            y = _silu(_layernorm(c, clng_ref[...], clnb_ref[...]))
            yconv_scr[r0:r0 + rows, :] = y.astype(jnp.bfloat16)
            return _token_of(y)
        return unit

    def sgu_norm(r0, rows):
        def unit(token):
            v = _after_f32(mix_scr[r0:r0 + rows, :], token)
            y = _layernorm(v, lng_ref[...], lnb_ref[...])
            vb_scr[r0:r0 + rows, :] = y.astype(jnp.bfloat16)
            return _token_of(y)
        return unit

    def window_sum(cols, win, token):
        first = {win: 0}
        k = win // 2
        while k >= 1:
            first[k] = -(-(k - first[2 * k]) // SUBLANES) * -SUBLANES
            k //= 2
        src, src_first, k = pbuf, -HALO, 1
        src_cols = cols
        while True:
            lo = first[2 * k] - src_first
            n = tm - first[2 * k]
            s = _after_f32(src[lo:lo + n, src_cols], token) + src[lo - k:lo - k + n, src_cols]
            token = None
            if 2 * k == win:
                return s
            slot = pool_lvl.at[(2 * k).bit_length() % 2]
            slot[0:n, :] = s
            src, src_first, src_cols, k = slot, first[2 * k], slice(None), 2 * k

    def pool_window(i):
        def unit(token):
            win = POOL_WINDOWS[i]
            cols = slice(i * POOL_GC, (i + 1) * POOL_GC)
            pos = s_id * tm + lax.broadcasted_iota(jnp.int32, (tm, 1), 0) + 1
            inv_cnt = 1.0 / jnp.minimum(pos, win).astype(jnp.float32)
            pooled = window_sum(cols, win, token) * inv_cnt - pbuf[body, cols]
            pooled_scr[:, cols] = pooled.astype(jnp.bfloat16)
            return _token_of(pooled)
        return unit

    def pool_history(token):
        p = _after_f32(pbuf[tm:tm + HALO, :], token)
        pbuf[0:HALO, :] = p
        return _token_of(p)


    matmul_units = []
    for j in range(N_BLK):
        matmul_units += [(glu_value(j), 0.1), (glu_gate(j), 0.45)]
    slot_glu_done = {j: 2 * j + 1 for j in range(N_BLK)}
    matmul_units += [(sgu_v(j), 0.4) for j in range(N_BLK)]
    slot_v_done = len(matmul_units) - 1
    matmul_units += [(pool_in(j), 0.1) for j in range(N_BLK)]
    slot_p_done = len(matmul_units) - 1
    for i in range(N_BRANCH):
        matmul_units += [(branch_gate(i, j), 0.35) for j in range(N_BLK)]
    slot_sgu_mix = len(matmul_units)
    matmul_units += [(sgu_mix, 0.3)]
    matmul_units += [(sgu_u(j), 0.5) for j in range(N_BLK)]
    matmul_units += [(branch_out(0, sa_scr, wpa_ref, j), 0.25) for j in range(N_BLK)]
    slot_pool_mix = len(matmul_units)
    matmul_units += [(pool_mix, 0.3)]
    matmul_units += [(branch_out(1, ypool_scr, wpb_ref, j), 0.3) for j in range(N_BLK)]
    slot_conv_out = len(matmul_units)
    matmul_units += [(branch_out(2, yconv_scr, wpc_ref, j), 0.3) for j in range(N_BLK)]
    matmul_units += [(out_block(j), 0.2) for j in range(N_BLK)]
    n_slots = len(matmul_units)

    vector_units = [[] for _ in range(n_slots)]
    load = [cost for _, cost in matmul_units]

    def place(unit, cost, earliest, latest):
        slots = range(earliest, latest + 1)
        slot = next((s for s in slots if load[s] + cost <= SLOT_VECTOR_BUDGET_PCT / 100),
                    min(slots, key=lambda s: load[s]))
        vector_units[slot].append(unit)
        load[slot] += cost
        return slot

    half = tm // 2
    full = 256 / BLK
    s = place(sgu_norm(0, half), 0.7 * full, slot_v_done + 1, slot_sgu_mix - 2)
    place(sgu_norm(half, half), 0.7 * full, s, slot_sgu_mix - 1)
    s = slot_p_done + 1
    for i, cost in enumerate((0.25, 0.5, 0.75, 1.0)):
        s = place(pool_window(i), cost * full, s, slot_pool_mix - 2)
    place(pool_history, 0.1 * full, s, slot_pool_mix - 1)

    s = 0
    for j in range(N_BLK):
        s = max(s, slot_glu_done[j] + 1)
        for shift in range(1, SUBLANES):
            s = place(shifted_copy(j, shift), 0.18, s, slot_conv_out - 4)
        for r0 in range(0, tm, CONV_ROWS):
            s = place(conv_unit(j, r0), 0.5 * CONV_ROWS / 32, s, slot_conv_out - 3)
    s = place(conv_history, 0.1 * full, s, slot_conv_out - 3)
    s = place(conv_norm(0, half), 1.5 * full, s, slot_conv_out - 2)
    place(conv_norm(half, half), 1.5 * full, s, slot_conv_out - 1)

    matmul_tokens, vector_tokens = {}, {}
    for s in range(n_slots):
        matmul_tokens[s] = matmul_units[s][0](vector_tokens.get(s - SCHED_LAG))
        tok = vector_tokens.get(s - 1)
        for unit in vector_units[s]:
            tok = unit(matmul_tokens.get(s - SCHED_LAG))
        vector_tokens[s] = tok


def _layer_resident(stacked, l):
    tail = (0,) * (stacked.ndim - 1)
    return pl.BlockSpec((None,) + stacked.shape[1:], lambda b, s: (l,) + tail,
                        pipeline_mode=pl.Buffered(1))


def _ada_spec(l, d):
    return pl.BlockSpec((None, None, 6, d), lambda b, s: (l, b, 0, 0))


def _mixer_operands(g_mix, w_in, b_in, sgu_ln_g, sgu_ln_b, sgu_w_s, sgu_b_s, w_pa, pool_w,
                    pool_scale, w_pb, conv_w, conv_b, conv_ln_g, conv_ln_b, w_pc, w_out):
    L = w_in.shape[0]
    row = lambda a: a.reshape(L, 1, -1)
    win_p, wpa_p, wpb_p, wpc_p, wout_p = (_pack_bf16_rows(w) for w in (w_in, w_pa, w_pb, w_pc, w_out))
    pool_p = _pack_bf16_rows(pool_w.reshape(L, POOL_GROUPS * POOL_GC, POOL_GC))
    pool_p = pool_p.reshape(L, POOL_GROUPS, POOL_GC // 2, POOL_GC)
    conv_w8 = jnp.repeat(conv_w, SUBLANES, axis=1)
    return (row(g_mix), win_p, row(b_in), row(sgu_ln_g), row(sgu_ln_b), sgu_w_s,
            jnp.swapaxes(sgu_b_s, 1, 2), wpa_p, pool_p, row(pool_scale), wpb_p,
            conv_w8, row(conv_b), row(conv_ln_g), row(conv_ln_b), wpc_p, wout_p)


def _mixer_layer(x, ada, operands, l):
    B, S, D = x.shape
    tm = TM_MIX
    tile = pl.BlockSpec((None, tm, D), lambda b, s: (b, s, 0))
    bf16_tile = pltpu.VMEM((tm, D), jnp.bfloat16)
    return pl.pallas_call(
        _mixer_kernel,
        grid=(B, S // tm),
        in_specs=[tile, _ada_spec(l, D)] + [_layer_resident(a, l) for a in operands],
        out_specs=tile,
        out_shape=jax.ShapeDtypeStruct(x.shape, x.dtype),
        scratch_shapes=[
            pltpu.VMEM((SGU_GROUPS, CHUNK, CHUNK), jnp.bfloat16),
            bf16_tile,
            bf16_tile,
            bf16_tile,
            bf16_tile,
            bf16_tile,
            bf16_tile,
            bf16_tile,
            pltpu.VMEM((tm, D), jnp.float32),
            pltpu.VMEM((HALO + tm, D), jnp.float32),
            pltpu.VMEM((2, tm + HALO - SUBLANES, POOL_GC), jnp.float32),
            pltpu.VMEM((HALO + tm, D), jnp.float32),
            pltpu.VMEM((SUBLANES - 1, tm + SHIFT_ROWS_EXTRA, D), jnp.float32),
            pltpu.VMEM((tm, D), jnp.float32),
            pltpu.VMEM((N_BRANCH, tm, D), jnp.float32),
            pltpu.VMEM((tm, D), jnp.float32),
        ],
        compiler_params=pltpu.CompilerParams(
            dimension_semantics=("arbitrary", "arbitrary"), vmem_limit_bytes=VMEM_LIMIT_BYTES),
        name="mixer_layer",
    )(x, ada, *operands)


def _ffn_kernel(x_ref, ada_ref, g_ref, win_ref, wout_ref, gfin_ref, o_ref, *, final_norm):
    x = x_ref[...]
    h = _modulated_rmsnorm(x, g_ref[...], ada_ref[ADA_SC_F:ADA_SC_F + 1, :],
                           ada_ref[ADA_SH_F:ADA_SH_F + 1, :]).astype(jnp.bfloat16)
    g_part = _wdot(h, win_ref[:, 0:D_FF])
    u_part = _wdot(h, win_ref[:, D_FF:2 * D_FF])
    act = (_silu(g_part) * u_part).astype(jnp.bfloat16)
    y = x + ada_ref[ADA_GT_F:ADA_GT_F + 1, :] * _wdot(act, wout_ref[...])
    if final_norm:
        y = y * lax.rsqrt(jnp.mean(y * y, axis=-1, keepdims=True) + EPS) * gfin_ref[...]
    o_ref[...] = y


def _ffn_operands(g_ffn, w_ffn_in, w_ffn_out, g_final):
    L = w_ffn_in.shape[0]
    return (g_ffn.reshape(L, 1, -1), _pack_bf16_rows(w_ffn_in), _pack_bf16_rows(w_ffn_out),
            jnp.broadcast_to(g_final.reshape(1, 1, -1), (L, 1, g_final.shape[-1])))


def _ffn_layer(x, ada, operands, l, final_norm):
    B, S, D = x.shape
    tm = TM_FFN
    tile = pl.BlockSpec((None, tm, D), lambda b, s: (b, s, 0))
    return pl.pallas_call(
        functools.partial(_ffn_kernel, final_norm=final_norm),
        grid=(B, S // tm),
        in_specs=[tile, _ada_spec(l, D)] + [_layer_resident(a, l) for a in operands],
        out_specs=tile,
        out_shape=jax.ShapeDtypeStruct(x.shape, x.dtype),
        compiler_params=pltpu.CompilerParams(
            dimension_semantics=("arbitrary", "arbitrary"), vmem_limit_bytes=VMEM_LIMIT_BYTES),
        name="ffn_layer",
    )(x, ada, *operands)


def kernel(x, c, w_ada, b_ada, g_mix, w_in, b_in, sgu_ln_g, sgu_ln_b, sgu_w_s, sgu_b_s, w_pa, pool_w, pool_scale, w_pb, conv_w, conv_b, conv_ln_g, conv_ln_b, w_pc, w_out, g_ffn, w_ffn_in, w_ffn_out, g_final):
    B, S, D = x.shape
    assert D == D_MODEL and S % TM_MIX == 0 and S % TM_FFN == 0 and TM_MIX % CHUNK == 0
    ada = _ada_table(c, w_ada, b_ada).reshape(DEPTH, B, 6, D)
    mixer_ops = _mixer_operands(g_mix, w_in, b_in, sgu_ln_g, sgu_ln_b, sgu_w_s, sgu_b_s, w_pa,
                                pool_w, pool_scale, w_pb, conv_w, conv_b, conv_ln_g, conv_ln_b,
                                w_pc, w_out)
    ffn_ops = _ffn_operands(g_ffn, w_ffn_in, w_ffn_out, g_final)
    for l in range(DEPTH):
        x = _mixer_layer(x, ada, mixer_ops, l)
        x = _ffn_layer(x, ada, ffn_ops, l, final_norm=(l == DEPTH - 1))
    return x
```

```python
import functools

import jax
import jax.numpy as jnp
from jax import lax
from jax.experimental import pallas as pl
from jax.experimental.pallas import tpu as pltpu

D_MODEL = 1024
DEPTH = 2
CHUNK = 128
SGU_GROUPS = 8
SGU_GC = D_MODEL // SGU_GROUPS
POOL_WINDOWS = (2, 4, 8, 16)
POOL_GROUPS = len(POOL_WINDOWS)
POOL_GC = D_MODEL // POOL_GROUPS
CONV_WIDTH = 31
D_FF = 2816
EPS = 1e-6

OFF_U = 0
OFF_V = D_MODEL
OFF_POOL = 2 * D_MODEL
OFF_GLU_A = 3 * D_MODEL
OFF_GLU_G = 4 * D_MODEL
OFF_GATE = 5 * D_MODEL
D_IN = 8 * D_MODEL
N_BRANCH = 3

SUBLANES = 8
LANES = 128
BF16_ROWS = 16
PACK_BLOCK_ROWS = 1024
PACK_BLOCK_COLS = 2048
PACK_VMEM_LIMIT_BYTES = 40 * 1024 * 1024
ADA_BLOCK_COLS = 2048
HALO = 32
SHIFT_ROWS_EXTRA = HALO - SUBLANES
TM_MIX = 256
TM_FFN = 512
BLK = 256
N_BLK = D_MODEL // BLK
CONV_ROWS = 32
SCHED_LAG = 3
SLOT_VECTOR_BUDGET_PCT = 120
VMEM_LIMIT_BYTES = 58 * 1024 * 1024

ADA_SH_M, ADA_SC_M, ADA_GT_M, ADA_SH_F, ADA_SC_F, ADA_GT_F = range(6)


def _pack_kernel(w_ref, o_ref):
    o_ref[...] = pltpu.bitcast(w_ref[...].astype(jnp.bfloat16), jnp.uint32)


def _largest_divisor(n, multiple_of, at_most):
    return max(d for d in range(multiple_of, at_most + 1, multiple_of) if n % d == 0)


def _pack_bf16_rows(w):
    L, K, N = w.shape
    kb = _largest_divisor(K, 2 * BF16_ROWS, PACK_BLOCK_ROWS)
    nb = _largest_divisor(N, LANES, PACK_BLOCK_COLS)
    return pl.pallas_call(
        _pack_kernel,
        grid=(L, K // kb, N // nb),
        in_specs=[pl.BlockSpec((None, kb, nb), lambda l, i, j: (l, i, j))],
        out_specs=pl.BlockSpec((None, kb // 2, nb), lambda l, i, j: (l, i, j)),
        out_shape=jax.ShapeDtypeStruct((L, K // 2, N), jnp.uint32),
        compiler_params=pltpu.CompilerParams(
            dimension_semantics=("arbitrary", "arbitrary", "arbitrary"),
            vmem_limit_bytes=PACK_VMEM_LIMIT_BYTES),
        name="pack_bf16_rows",
    )(w)


def _wdot(a, w_packed):
    return jnp.dot(a, pltpu.bitcast(w_packed, jnp.bfloat16), preferred_element_type=jnp.float32)


def _sigmoid(x):
    return 0.5 * jnp.tanh(0.5 * x) + 0.5


def _silu(x):
    hx = 0.5 * x
    return hx * jnp.tanh(hx) + hx


def _gelu(x):
    hx = 0.5 * x
    return hx * lax.erf(x * (2.0 ** -0.5)) + hx


def _layernorm(x, g, b):
    mu = jnp.mean(x, axis=-1, keepdims=True)
    xc = x - mu
    var = jnp.mean(xc * xc, axis=-1, keepdims=True)
    return xc * lax.rsqrt(var + EPS) * g + b


def _modulated_rmsnorm(x, g, scale, shift):
    y = x * lax.rsqrt(jnp.mean(x * x, axis=-1, keepdims=True) + EPS) * g
    return y * (1.0 + scale) + shift


def _token_zero(token):
    bits = pltpu.bitcast(token, jnp.uint32)
    bits = lax.shift_right_logical(lax.shift_right_logical(bits, jnp.uint32(16)), jnp.uint32(16))
    return pltpu.bitcast(bits, jnp.float32)


def _after_f32(val, token):
    if token is None:
        return val
    head = val[0:SUBLANES, 0:LANES] + _token_zero(token)
    if val.shape[1] > LANES:
        head = jnp.concatenate([head, val[0:SUBLANES, LANES:]], axis=1)
    if val.shape[0] > SUBLANES:
        head = jnp.concatenate([head, val[SUBLANES:]], axis=0)
    return head


def _after_bf16(val, token):
    if token is None:
        return val
    zero = _token_zero(token)
    zero = jnp.concatenate([zero, zero], axis=0).astype(jnp.bfloat16)
    head = val[0:BF16_ROWS, 0:LANES] + zero
    if val.shape[1] > LANES:
        head = jnp.concatenate([head, val[0:BF16_ROWS, LANES:]], axis=1)
    if val.shape[0] > BF16_ROWS:
        head = jnp.concatenate([head, val[BF16_ROWS:]], axis=0)
    return head


def _token_of(val):
    return val[val.shape[0] - SUBLANES:, val.shape[1] - LANES:]


def _ada_kernel(c_ref, w_ref, b_ref, o_ref):
    c = c_ref[...]
    c_act = c / (1.0 + jnp.exp(-c))
    o_ref[...] = jnp.dot(c_act, w_ref[...], preferred_element_type=jnp.float32,
                         precision=lax.Precision.HIGHEST) + b_ref[...]


def _ada_table(c, w_ada, b_ada):
    L, D, D6 = w_ada.shape
    B = c.shape[0]
    nb = _largest_divisor(D6, LANES, ADA_BLOCK_COLS)
    return pl.pallas_call(
        _ada_kernel,
        grid=(L, D6 // nb),
        in_specs=[
            pl.BlockSpec((B, D), lambda l, j: (0, 0)),
            pl.BlockSpec((None, D, nb), lambda l, j: (l, 0, j)),
            pl.BlockSpec((None, 1, nb), lambda l, j: (l, 0, j)),
        ],
        out_specs=pl.BlockSpec((None, B, nb), lambda l, j: (l, 0, j)),
        out_shape=jax.ShapeDtypeStruct((L, B, D6), jnp.float32),
        compiler_params=pltpu.CompilerParams(
            dimension_semantics=("arbitrary", "arbitrary"),
            vmem_limit_bytes=PACK_VMEM_LIMIT_BYTES),
        name="ada_table",
    )(c, w_ada, b_ada.reshape(L, 1, D6))


def _mixer_kernel(x_ref, ada_ref, gmix_ref, win_ref, bin_ref, lng_ref, lnb_ref, ws_ref, bs_ref,
                  wpa_ref, poolw_ref, pscale_ref, wpb_ref, convw_ref, convb_ref, clng_ref,
                  clnb_ref, wpc_ref, wout_ref, o_ref,
                  wm_scr, h_scr, vb_scr, sa_scr, pooled_scr, ypool_scr, yconv_scr, merged_scr,
                  mix_scr, pbuf, pool_lvl, zbuf, zsh, conv_scr, gate_scr, acc_scr):
    tm = x_ref.shape[0]
    b_id = pl.program_id(0)
    s_id = pl.program_id(1)

    @pl.when(jnp.logical_and(b_id == 0, s_id == 0))
    def _():
        row = lax.broadcasted_iota(jnp.int32, (CHUNK, CHUNK), 0)
        col = lax.broadcasted_iota(jnp.int32, (CHUNK, CHUNK), 1)
        for g in range(SGU_GROUPS):
            wm_scr[g] = jnp.where(row >= col, ws_ref[g], 0.0).astype(jnp.bfloat16)

    @pl.when(s_id == 0)
    def _():
        pbuf[0:HALO, :] = jnp.zeros((HALO, D_MODEL), jnp.float32)
        zbuf[0:HALO, :] = jnp.zeros((HALO, D_MODEL), jnp.float32)

    h_scr[...] = _modulated_rmsnorm(
        x_ref[...], gmix_ref[...], ada_ref[ADA_SC_M:ADA_SC_M + 1, :],
        ada_ref[ADA_SH_M:ADA_SH_M + 1, :]).astype(jnp.bfloat16)

    blk = lambda j: slice(j * BLK, (j + 1) * BLK)
    body = slice(HALO, HALO + tm)


    def proj_block(off, j, token):
        lo = off + j * BLK
        r = _wdot(_after_bf16(h_scr[...], token), win_ref[:, lo:lo + BLK])
        return r + bin_ref[:, lo:lo + BLK]

    def glu_value(j):
        def unit(token):
            a = proj_block(OFF_GLU_A, j, token)
            conv_scr[:, blk(j)] = a
            return _token_of(a)
        return unit

    def glu_gate(j):
        def unit(token):
            g = proj_block(OFF_GLU_G, j, token)
            z = conv_scr[:, blk(j)] * _sigmoid(g)
            zbuf[body, blk(j)] = z
            return _token_of(z)
        return unit

    def sgu_v(j):
        def unit(token):
            v = _gelu(proj_block(OFF_V, j, token))
            mix_scr[:, blk(j)] = v
            return _token_of(v)
        return unit

    def sgu_mix(token):
        n_chunks = tm // CHUNK
        tok = None
        for g in range(SGU_GROUPS):
            cols = slice(g * SGU_GC, (g + 1) * SGU_GC)
            rhs = jnp.concatenate(
                [vb_scr[n * CHUNK:(n + 1) * CHUNK, cols] for n in range(n_chunks)], axis=1)
            lhs = _after_bf16(wm_scr[g], token if g == 0 else None)
            mixed = jnp.dot(lhs, rhs, preferred_element_type=jnp.float32) + bs_ref[:, g:g + 1]
            for n in range(n_chunks):
                mix_scr[n * CHUNK:(n + 1) * CHUNK, cols] = mixed[:, n * SGU_GC:(n + 1) * SGU_GC]
            tok = _token_of(mixed)
        return tok

    def sgu_u(j):
        def unit(token):
            s = _gelu(proj_block(OFF_U, j, token)) * mix_scr[:, blk(j)]
            sa_scr[:, blk(j)] = s.astype(jnp.bfloat16)
            return _token_of(s)
        return unit

    def pool_in(j):
        def unit(token):
            p = proj_block(OFF_POOL, j, token)
            pbuf[body, blk(j)] = p
            return _token_of(p)
        return unit

    def branch_gate(i, j):
        def unit(token):
            g = _sigmoid(proj_block(OFF_GATE + i * D_MODEL, j, token))
            gate_scr[i, :, blk(j)] = g
            return _token_of(g)
        return unit

    def pool_mix(token):
        for i in range(POOL_GROUPS):
            cols = slice(i * POOL_GC, (i + 1) * POOL_GC)
            lhs = _after_bf16(pooled_scr[:, cols], token if i == 0 else None)
            y = _wdot(lhs, poolw_ref[i]) * pscale_ref[:, cols]
            ypool_scr[:, cols] = y.astype(jnp.bfloat16)
        return _token_of(y)

    def branch_out(i, src_scr, w_ref, j):
        def unit(token):
            y = gate_scr[i, :, blk(j)] * _wdot(_after_bf16(src_scr[...], token), w_ref[:, blk(j)])
            if i == 0:
                acc_scr[:, blk(j)] = y
            elif i < N_BRANCH - 1:
                acc_scr[:, blk(j)] = acc_scr[:, blk(j)] + y
            else:
                y = acc_scr[:, blk(j)] + y
                merged_scr[:, blk(j)] = y.astype(jnp.bfloat16)
            return _token_of(y)
        return unit

    def out_block(j):
        def unit(token):
            y = _wdot(_after_bf16(merged_scr[...], token), wout_ref[:, blk(j)])
            y = x_ref[:, blk(j)] + ada_ref[ADA_GT_M:ADA_GT_M + 1, blk(j)] * y
            o_ref[:, blk(j)] = y
            return _token_of(y)
        return unit


    def shifted_copy(j, shift):
        def unit(token):
            n = tm + SHIFT_ROWS_EXTRA
            if token is not None:
                first = (slice(HALO, HALO + SUBLANES), blk(j))
                zbuf[first] = _after_f32(zbuf[first], token)
            zsh[shift - 1, :, blk(j)] = zbuf[shift:shift + n, blk(j)]
            return zsh[shift - 1, n - SUBLANES:n, (j + 1) * BLK - LANES:(j + 1) * BLK]
        return unit

    def conv_unit(j, r0):
        def unit(token):
            acc = jnp.zeros((CONV_ROWS, BLK), jnp.float32) + convb_ref[:, blk(j)]
            for k in range(CONV_WIDTH):
                off = HALO - (CONV_WIDTH - 1) + k
                shift = off % SUBLANES
                base = off - shift + r0
                if shift == 0:
                    src = zbuf[base:base + CONV_ROWS, blk(j)]
                else:
                    src = zsh[shift - 1, base:base + CONV_ROWS, blk(j)]
                if k == 0:
                    src = _after_f32(src, token)
                w8 = convw_ref[SUBLANES * k:SUBLANES * (k + 1), blk(j)]
                prod = src.reshape(CONV_ROWS // SUBLANES, SUBLANES, BLK) * w8[None]
                acc = acc + prod.reshape(CONV_ROWS, BLK)
            conv_scr[r0:r0 + CONV_ROWS, blk(j)] = acc
            return _token_of(acc)
        return unit

    def conv_history(token):
        z = _after_f32(zbuf[tm:tm + HALO, :], token)
        zbuf[0:HALO, :] = z
        return _token_of(z)

    def conv_norm(r0, rows):
        def unit(token):
            c = _after_f32(conv_scr[r0:r0 + rows, :], token)
            y = _silu(_layernorm(c, clng_ref[...], clnb_ref[...]))
            yconv_scr[r0:r0 + rows, :] = y.astype(jnp.bfloat16)
            return _token_of(y)
        return unit

    def sgu_norm(r0, rows):
        def unit(token):
            v = _after_f32(mix_scr[r0:r0 + rows, :], token)
            y = _layernorm(v, lng_ref[...], lnb_ref[...])
            vb_scr[r0:r0 + rows, :] = y.astype(jnp.bfloat16)
            return _token_of(y)
        return unit

    def window_sum(cols, win, token):
        first = {win: 0}
        k = win // 2
        while k >= 1:
            first[k] = -(-(k - first[2 * k]) // SUBLANES) * -SUBLANES
            k //= 2
        src, src_first, k = pbuf, -HALO, 1
        src_cols = cols
        while True:
            lo = first[2 * k] - src_first
            n = tm - first[2 * k]
            s = _after_f32(src[lo:lo + n, src_cols], token) + src[lo - k:lo - k + n, src_cols]
            token = None
            if 2 * k == win:
                return s
            slot = pool_lvl.at[(2 * k).bit_length() % 2]
            slot[0:n, :] = s
            src, src_first, src_cols, k = slot, first[2 * k], slice(None), 2 * k

    def pool_window(i):
        def unit(token):
            win = POOL_WINDOWS[i]
            cols = slice(i * POOL_GC, (i + 1) * POOL_GC)
            pos = s_id * tm + lax.broadcasted_iota(jnp.int32, (tm, 1), 0) + 1
            inv_cnt = 1.0 / jnp.minimum(pos, win).astype(jnp.float32)
            pooled = window_sum(cols, win, token) * inv_cnt - pbuf[body, cols]
            pooled_scr[:, cols] = pooled.astype(jnp.bfloat16)
            return _token_of(pooled)
        return unit

    def pool_history(token):
        p = _after_f32(pbuf[tm:tm + HALO, :], token)
        pbuf[0:HALO, :] = p
        return _token_of(p)


    matmul_units = []
    for j in range(N_BLK):
        matmul_units += [(glu_value(j), 0.1), (glu_gate(j), 0.45)]
    slot_glu_done = {j: 2 * j + 1 for j in range(N_BLK)}
    matmul_units += [(sgu_v(j), 0.4) for j in range(N_BLK)]
    slot_v_done = len(matmul_units) - 1
    matmul_units += [(pool_in(j), 0.1) for j in range(N_BLK)]
    slot_p_done = len(matmul_units) - 1
    for i in range(N_BRANCH):
        matmul_units += [(branch_gate(i, j), 0.35) for j in range(N_BLK)]
    slot_sgu_mix = len(matmul_units)
    matmul_units += [(sgu_mix, 0.3)]
    matmul_units += [(sgu_u(j), 0.5) for j in range(N_BLK)]
    matmul_units += [(branch_out(0, sa_scr, wpa_ref, j), 0.25) for j in range(N_BLK)]
    slot_pool_mix = len(matmul_units)
    matmul_units += [(pool_mix, 0.3)]
    matmul_units += [(branch_out(1, ypool_scr, wpb_ref, j), 0.3) for j in range(N_BLK)]
    slot_conv_out = len(matmul_units)
    matmul_units += [(branch_out(2, yconv_scr, wpc_ref, j), 0.3) for j in range(N_BLK)]
    matmul_units += [(out_block(j), 0.2) for j in range(N_BLK)]
    n_slots = len(matmul_units)

    vector_units = [[] for _ in range(n_slots)]
    load = [cost for _, cost in matmul_units]

    def place(unit, cost, earliest, latest):
        slots = range(earliest, latest + 1)
        slot = next((s for s in slots if load[s] + cost <= SLOT_VECTOR_BUDGET_PCT / 100),
                    min(slots, key=lambda s: load[s]))
        vector_units[slot].append(unit)
        load[slot] += cost
        return slot

    half = tm // 2
    full = 256 / BLK
    s = place(sgu_norm(0, half), 0.7 * full, slot_v_done + 1, slot_sgu_mix - 2)
    place(sgu_norm(half, half), 0.7 * full, s, slot_sgu_mix - 1)
    s = slot_p_done + 1
    for i, cost in enumerate((0.25, 0.5, 0.75, 1.0)):
        s = place(pool_window(i), cost * full, s, slot_pool_mix - 2)
    place(pool_history, 0.1 * full, s, slot_pool_mix - 1)

    s = 0
    for j in range(N_BLK):
        s = max(s, slot_glu_done[j] + 1)
        for shift in range(1, SUBLANES):
            s = place(shifted_copy(j, shift), 0.18, s, slot_conv_out - 4)
        for r0 in range(0, tm, CONV_ROWS):
            s = place(conv_unit(j, r0), 0.5 * CONV_ROWS / 32, s, slot_conv_out - 3)
    s = place(conv_history, 0.1 * full, s, slot_conv_out - 3)
    s = place(conv_norm(0, half), 1.5 * full, s, slot_conv_out - 2)
    place(conv_norm(half, half), 1.5 * full, s, slot_conv_out - 1)

    matmul_tokens, vector_tokens = {}, {}
    for s in range(n_slots):
        matmul_tokens[s] = matmul_units[s][0](vector_tokens.get(s - SCHED_LAG))
        tok = vector_tokens.get(s - 1)
        for unit in vector_units[s]:
            tok = unit(matmul_tokens.get(s - SCHED_LAG))
        vector_tokens[s] = tok


def _layer_resident(stacked, l):
    tail = (0,) * (stacked.ndim - 1)
    return pl.BlockSpec((None,) + stacked.shape[1:], lambda b, s: (l,) + tail,
                        pipeline_mode=pl.Buffered(1))


def _ada_spec(l, d):
    return pl.BlockSpec((None, None, 6, d), lambda b, s: (l, b, 0, 0))


def _mixer_operands(g_mix, w_in, b_in, sgu_ln_g, sgu_ln_b, sgu_w_s, sgu_b_s, w_pa, pool_w,
                    pool_scale, w_pb, conv_w, conv_b, conv_ln_g, conv_ln_b, w_pc, w_out):
    L = w_in.shape[0]
    row = lambda a: a.reshape(L, 1, -1)
    win_p, wpa_p, wpb_p, wpc_p, wout_p = (_pack_bf16_rows(w) for w in (w_in, w_pa, w_pb, w_pc, w_out))
    pool_p = _pack_bf16_rows(pool_w.reshape(L, POOL_GROUPS * POOL_GC, POOL_GC))
    pool_p = pool_p.reshape(L, POOL_GROUPS, POOL_GC // 2, POOL_GC)
    conv_w8 = jnp.repeat(conv_w, SUBLANES, axis=1)
    return (row(g_mix), win_p, row(b_in), row(sgu_ln_g), row(sgu_ln_b), sgu_w_s,
            jnp.swapaxes(sgu_b_s, 1, 2), wpa_p, pool_p, row(pool_scale), wpb_p,
            conv_w8, row(conv_b), row(conv_ln_g), row(conv_ln_b), wpc_p, wout_p)


def _mixer_layer(x, ada, operands, l):
    B, S, D = x.shape
    tm = TM_MIX
    tile = pl.BlockSpec((None, tm, D), lambda b, s: (b, s, 0))
    bf16_tile = pltpu.VMEM((tm, D), jnp.bfloat16)
    return pl.pallas_call(
        _mixer_kernel,
        grid=(B, S // tm),
        in_specs=[tile, _ada_spec(l, D)] + [_layer_resident(a, l) for a in operands],
        out_specs=tile,
        out_shape=jax.ShapeDtypeStruct(x.shape, x.dtype),
        scratch_shapes=[
            pltpu.VMEM((SGU_GROUPS, CHUNK, CHUNK), jnp.bfloat16),
            bf16_tile,
            bf16_tile,
            bf16_tile,
            bf16_tile,
            bf16_tile,
            bf16_tile,
            bf16_tile,
            pltpu.VMEM((tm, D), jnp.float32),
            pltpu.VMEM((HALO + tm, D), jnp.float32),
            pltpu.VMEM((2, tm + HALO - SUBLANES, POOL_GC), jnp.float32),
            pltpu.VMEM((HALO + tm, D), jnp.float32),
            pltpu.VMEM((SUBLANES - 1, tm + SHIFT_ROWS_EXTRA, D), jnp.float32),
            pltpu.VMEM((tm, D), jnp.float32),
            pltpu.VMEM((N_BRANCH, tm, D), jnp.float32),
            pltpu.VMEM((tm, D), jnp.float32),
        ],
        compiler_params=pltpu.CompilerParams(
            dimension_semantics=("arbitrary", "arbitrary"), vmem_limit_bytes=VMEM_LIMIT_BYTES),
        name="mixer_layer",
    )(x, ada, *operands)


def _ffn_kernel(x_ref, ada_ref, g_ref, win_ref, wout_ref, gfin_ref, o_ref, *, final_norm):
    x = x_ref[...]
    h = _modulated_rmsnorm(x, g_ref[...], ada_ref[ADA_SC_F:ADA_SC_F + 1, :],
                           ada_ref[ADA_SH_F:ADA_SH_F + 1, :]).astype(jnp.bfloat16)
    g_part = _wdot(h, win_ref[:, 0:D_FF])
    u_part = _wdot(h, win_ref[:, D_FF:2 * D_FF])
    act = (_silu(g_part) * u_part).astype(jnp.bfloat16)
    y = x + ada_ref[ADA_GT_F:ADA_GT_F + 1, :] * _wdot(act, wout_ref[...])
    if final_norm:
        y = y * lax.rsqrt(jnp.mean(y * y, axis=-1, keepdims=True) + EPS) * gfin_ref[...]
    o_ref[...] = y


def _ffn_operands(g_ffn, w_ffn_in, w_ffn_out, g_final):
    L = w_ffn_in.shape[0]
    return (g_ffn.reshape(L, 1, -1), _pack_bf16_rows(w_ffn_in), _pack_bf16_rows(w_ffn_out),
            jnp.broadcast_to(g_final.reshape(1, 1, -1), (L, 1, g_final.shape[-1])))


def _ffn_layer(x, ada, operands, l, final_norm):
    B, S, D = x.shape
    tm = TM_FFN
    tile = pl.BlockSpec((None, tm, D), lambda b, s: (b, s, 0))
    return pl.pallas_call(
        functools.partial(_ffn_kernel, final_norm=final_norm),
        grid=(B, S // tm),
        in_specs=[tile, _ada_spec(l, D)] + [_layer_resident(a, l) for a in operands],
        out_specs=tile,
        out_shape=jax.ShapeDtypeStruct(x.shape, x.dtype),
        compiler_params=pltpu.CompilerParams(
            dimension_semantics=("arbitrary", "arbitrary"), vmem_limit_bytes=VMEM_LIMIT_BYTES),
        name="ffn_layer",
    )(x, ada, *operands)


def kernel(x, c, w_ada, b_ada, g_mix, w_in, b_in, sgu_ln_g, sgu_ln_b, sgu_w_s, sgu_b_s, w_pa, pool_w, pool_scale, w_pb, conv_w, conv_b, conv_ln_g, conv_ln_b, w_pc, w_out, g_ffn, w_ffn_in, w_ffn_out, g_final):
    B, S, D = x.shape
    assert D == D_MODEL and S % TM_MIX == 0 and S % TM_FFN == 0 and TM_MIX % CHUNK == 0
    ada = _ada_table(c, w_ada, b_ada).reshape(DEPTH, B, 6, D)
    mixer_ops = _mixer_operands(g_mix, w_in, b_in, sgu_ln_g, sgu_ln_b, sgu_w_s, sgu_b_s, w_pa,
                                pool_w, pool_scale, w_pb, conv_w, conv_b, conv_ln_g, conv_ln_b,
                                w_pc, w_out)
    ffn_ops = _ffn_operands(g_ffn, w_ffn_in, w_ffn_out, g_final)
    for l in range(DEPTH):
        x = _mixer_layer(x, ada, mixer_ops, l)
        x = _ffn_layer(x, ada, ffn_ops, l, final_norm=(l == DEPTH - 1))
    return x
```
